```python
import math
import jax, jax.numpy as jnp
from jax import lax
import numpy as np

D_MODEL = 2048
BATCH = 2
SEQ = 8192
DEPTH = 4
DEC_BATCH = 8
DEC_SEQ = 32
PAST_LEN = 1024

CHUNK = 64
LEFT_CHUNKS = 8
N_BAND = LEFT_CHUNKS + 1
ATTN_WIN = LEFT_CHUNKS * CHUNK
HEAD_DIM = 128
N_HEADS = D_MODEL // HEAD_DIM
REL_CLIP = 128
N_REL = 2 * REL_CLIP + 1
SSM_EXPAND = 2
D_INNER = SSM_EXPAND * D_MODEL
SSM_HEADDIM = 64
N_SSM_HEADS = D_INNER // SSM_HEADDIM
N_GROUPS = 8
HEADS_PER_GROUP = N_SSM_HEADS // N_GROUPS
D_STATE = 128
CONV_WIDTH = 4
CONV_DIM = D_INNER + 2 * N_GROUPS * D_STATE
IN_PROJ_DIM = D_INNER + CONV_DIM + N_SSM_HEADS
D_FF = (11 * D_MODEL) // 4
N_ATTN_LAYERS = (DEPTH + 1) // 2
N_SSD_LAYERS = DEPTH // 2
EPS = 1e-6
NEG_INF = -1e30

kernel_name = 'hybrid_stream_band_attn_ssd_macaron_step'


def _rms_norm(x, g):
    xf = x.astype(jnp.float32)
    y = xf * lax.rsqrt(jnp.mean(xf * xf, axis=-1, keepdims=True) + EPS)
    return (y * g.astype(jnp.float32)).astype(x.dtype)


def _swiglu_ffn(x, g, w_gate_up, w_down):
    gate, up = jnp.split(_rms_norm(x, g) @ w_gate_up, 2, axis=-1)
    return (jax.nn.silu(gate) * up) @ w_down


def _qkv(x, g, w_qkv, q_gain, k_gain):
    q, k, v = jnp.split(_rms_norm(x, g) @ w_qkv, 3, axis=-1)
    shape = x.shape[:-1] + (N_HEADS, HEAD_DIM)
    return _rms_norm(q.reshape(shape), q_gain), _rms_norm(k.reshape(shape), k_gain), v.reshape(shape)


def _rel_bias(table, dist):
    return table[:, jnp.clip(dist, -REL_CLIP, REL_CLIP) + REL_CLIP]


def _masked_softmax(s, bias, mask):
    s = s.astype(jnp.float32) + bias.astype(jnp.float32)
    return jax.nn.softmax(jnp.where(mask, s, NEG_INF), axis=-1)


def _attn_prompt(x, g, w_qkv, q_gain, k_gain, rel_table, w_o):
    b, L, _ = x.shape
    nc = L // CHUNK
    q, k, v = _qkv(x, g, w_qkv, q_gain, k_gain)
    qc = q.reshape(b, nc, CHUNK, N_HEADS, HEAD_DIM)
    pad = ((0, 0), (LEFT_CHUNKS, 0), (0, 0), (0, 0), (0, 0))
    kp = jnp.pad(k.reshape(b, nc, CHUNK, N_HEADS, HEAD_DIM), pad)
    vp = jnp.pad(v.reshape(b, nc, CHUNK, N_HEADS, HEAD_DIM), pad)
    scale = HEAD_DIM ** -0.5
    s = jnp.concatenate(
        [jnp.einsum('bcqhd,bckhd->bchqk', qc, kp[:, j:j + nc]) for j in range(N_BAND)],
        axis=-1) * scale
    q_rel = jnp.arange(CHUNK) + LEFT_CHUNKS * CHUNK
    k_rel = jnp.arange(N_BAND * CHUNK)
    bias = _rel_bias(rel_table, q_rel[:, None] - k_rel[None, :])
    valid = (jnp.arange(nc)[:, None] + k_rel[None, :] // CHUNK) >= LEFT_CHUNKS
    p = _masked_softmax(s, bias[None, None], valid[None, :, None, None, :]).astype(v.dtype)
    o = sum(jnp.einsum('bchqk,bckhd->bcqhd', p[..., j * CHUNK:(j + 1) * CHUNK], vp[:, j:j + nc])
            for j in range(N_BAND))
    y = o.reshape(b, L, N_HEADS * HEAD_DIM) @ w_o
    win = min(ATTN_WIN, L)
    return y, k[:, L - win:], v[:, L - win:]


def _attn_sample(x, k_cache, v_cache, g, w_qkv, q_gain, k_gain, rel_table, w_o):
    b, S, _ = x.shape
    q, k, v = _qkv(x, g, w_qkv, q_gain, k_gain)
    w = k_cache.shape[1]
    k_all = jnp.concatenate([k_cache.astype(k.dtype), k], axis=1)
    v_all = jnp.concatenate([v_cache.astype(v.dtype), v], axis=1)
    q_pos = PAST_LEN + jnp.arange(S)
    k_pos = jnp.concatenate([PAST_LEN - w + jnp.arange(w), q_pos])
    qch, kch = q_pos // CHUNK, k_pos // CHUNK
    mask = (kch[None, :] <= qch[:, None]) & (kch[None, :] >= qch[:, None] - LEFT_CHUNKS)
    bias = _rel_bias(rel_table, q_pos[:, None] - k_pos[None, :])
    s = jnp.einsum('bqhd,bkhd->bhqk', q, k_all) * (HEAD_DIM ** -0.5)
    p = _masked_softmax(s, bias[None], mask[None, None]).astype(v.dtype)
    o = jnp.einsum('bhqk,bkhd->bqhd', p, v_all)
    return o.reshape(b, S, N_HEADS * HEAD_DIM) @ w_o, k, v


def _ssd_scan(x, dt, A, B, C, h0):
    b, L, H, P = x.shape
    cl = min(CHUNK, L)
    nc = L // cl
    G, R = N_GROUPS, HEADS_PER_GROUP
    xdt = (x * dt[..., None].astype(x.dtype)).reshape(b, nc, cl, G, R, P)
    a_cs = jnp.cumsum((dt * A).reshape(b, nc, cl, G, R).transpose(0, 1, 3, 4, 2), axis=-1)
    tri = jnp.tril(jnp.ones((cl, cl), dtype=bool))
    seg = jnp.exp(jnp.where(tri, a_cs[..., :, None] - a_cs[..., None, :], -jnp.inf))
    Bc = B.reshape(b, nc, cl, G, D_STATE)
    Cc = C.reshape(b, nc, cl, G, D_STATE)
    cb = jnp.einsum('bclgn,bcsgn->bcgls', Cc, Bc)
    y_diag = jnp.einsum('bcgrls,bcsgrp->bclgrp', (cb[:, :, :, None] * seg).astype(x.dtype), xdt)
    decay_to_end = jnp.exp(a_cs[..., -1:] - a_cs).transpose(0, 1, 4, 2, 3)[..., None]
    states = jnp.einsum('bcsgn,bcsgrp->bcgrpn', Bc, xdt * decay_to_end.astype(x.dtype))
    chunk_decay = jnp.exp(a_cs[..., -1])

    def step(h, inp):
        st, dec = inp
        return h * dec[..., None, None] + st.astype(jnp.float32), h

    h_init = h0.reshape(b, G, R, P, D_STATE).astype(jnp.float32)
    h_final, h_start = lax.scan(step, h_init, (jnp.moveaxis(states, 1, 0), jnp.moveaxis(chunk_decay, 1, 0)))
    h_start = jnp.moveaxis(h_start, 0, 1).astype(x.dtype)
    decay_in = jnp.exp(a_cs).transpose(0, 1, 4, 2, 3)[..., None].astype(x.dtype)
    y_off = jnp.einsum('bclgn,bcgrpn->bclgrp', Cc, h_start) * decay_in
    y = (y_diag + y_off).reshape(b, L, H, P)
    return y, h_final.reshape(b, H, P, D_STATE).astype(x.dtype)


def _ssd_mixer(x, h0, conv_buf, g, w_in, conv_w, conv_b, dt_bias, a_log, d_skip, gate_g, w_out):
    b, L, _ = x.shape
    z, xbc, dt = jnp.split(_rms_norm(x, g) @ w_in, [D_INNER, D_INNER + CONV_DIM], axis=-1)
    xbc_pad = jnp.concatenate([conv_buf.astype(xbc.dtype), xbc], axis=1)
    conv = conv_b + sum(xbc_pad[:, i:i + L] * conv_w[i] for i in range(CONV_WIDTH))
    new_buf = xbc_pad[:, L:]
    xs, B, C = jnp.split(jax.nn.silu(conv), [D_INNER, D_INNER + N_GROUPS * D_STATE], axis=-1)
    dt = jax.nn.softplus(dt.astype(jnp.float32) + dt_bias.astype(jnp.float32))
    A = -jnp.exp(a_log.astype(jnp.float32))
    xh = xs.reshape(b, L, N_SSM_HEADS, SSM_HEADDIM)
    y, h_final = _ssd_scan(xh, dt, A, B.reshape(b, L, N_GROUPS, D_STATE),
                           C.reshape(b, L, N_GROUPS, D_STATE), h0)
    y = (y + d_skip[:, None] * xh).reshape(b, L, D_INNER)
    y = _rms_norm(y * jax.nn.silu(z), gate_g)
    return y @ w_out, h_final, new_buf


def setup_inputs(seed: int = 0) -> dict:
    key = jax.random.key(seed)
    ks = jax.random.split(key, 24)
    f32 = jnp.float32

    def nrm(k, shape, scale):
        return jax.random.normal(k, shape, f32) * scale

    def gain(k, shape):
        return 1.0 + 0.02 * jax.random.normal(k, shape, f32)

    win = min(ATTN_WIN, PAST_LEN)
    dt0 = jnp.exp(jax.random.uniform(ks[19], (N_SSD_LAYERS, N_SSM_HEADS), f32,
                                     math.log(1e-3), math.log(1e-1)))
    return {
        'x_prompt': nrm(ks[0], (BATCH, SEQ, D_MODEL), 1.0),
        'x_sample': nrm(ks[1], (DEC_BATCH, DEC_SEQ, D_MODEL), 1.0),
        'cache_k': nrm(ks[2], (N_ATTN_LAYERS, DEC_BATCH, win, N_HEADS, HEAD_DIM), 1.0),
        'cache_v': nrm(ks[3], (N_ATTN_LAYERS, DEC_BATCH, win, N_HEADS, HEAD_DIM), 1.0),
        'state_ssm': nrm(ks[4], (N_SSD_LAYERS, DEC_BATCH, N_SSM_HEADS, SSM_HEADDIM, D_STATE), 0.1),
        'state_conv': nrm(ks[5], (N_SSD_LAYERS, DEC_BATCH, CONV_WIDTH - 1, CONV_DIM), 1.0),
        'ffn_norm': gain(ks[6], (DEPTH, 2, D_MODEL)),
        'ffn_w_gate_up': nrm(ks[7], (DEPTH, 2, D_MODEL, 2 * D_FF), D_MODEL ** -0.5),
        'ffn_w_down': nrm(ks[8], (DEPTH, 2, D_FF, D_MODEL), D_FF ** -0.5),
        'attn_norm': gain(ks[9], (N_ATTN_LAYERS, D_MODEL)),
        'attn_w_qkv': nrm(ks[10], (N_ATTN_LAYERS, D_MODEL, 3 * N_HEADS * HEAD_DIM), D_MODEL ** -0.5),
        'attn_q_norm': gain(ks[11], (N_ATTN_LAYERS, HEAD_DIM)),
        'attn_k_norm': gain(ks[12], (N_ATTN_LAYERS, HEAD_DIM)),
        'attn_rel_bias': nrm(ks[13], (N_ATTN_LAYERS, N_HEADS, N_REL), 0.5),
        'attn_w_o': nrm(ks[14], (N_ATTN_LAYERS, N_HEADS * HEAD_DIM, D_MODEL), (N_HEADS * HEAD_DIM) ** -0.5),
        'ssd_norm': gain(ks[15], (N_SSD_LAYERS, D_MODEL)),
        'ssd_w_in': nrm(ks[16], (N_SSD_LAYERS, D_MODEL, IN_PROJ_DIM), D_MODEL ** -0.5),
        'ssd_conv_w': nrm(ks[17], (N_SSD_LAYERS, CONV_WIDTH, CONV_DIM), CONV_WIDTH ** -0.5),
        'ssd_conv_b': nrm(ks[18], (N_SSD_LAYERS, CONV_DIM), 0.02),
        'ssd_dt_bias': dt0 + jnp.log(-jnp.expm1(-dt0)),
        'ssd_a_log': jnp.log(jax.random.uniform(ks[20], (N_SSD_LAYERS, N_SSM_HEADS), f32, 1.0, 16.0)),
        'ssd_d_skip': gain(ks[21], (N_SSD_LAYERS, N_SSM_HEADS)),
        'ssd_gate_norm': gain(ks[22], (N_SSD_LAYERS, D_INNER)),
        'ssd_w_out': nrm(ks[23], (N_SSD_LAYERS, D_INNER, D_MODEL), D_INNER ** -0.5),
    }


def reference(x_prompt, x_sample, cache_k, cache_v, state_ssm, state_conv,
              ffn_norm, ffn_w_gate_up, ffn_w_down,
              attn_norm, attn_w_qkv, attn_q_norm, attn_k_norm, attn_rel_bias, attn_w_o,
              ssd_norm, ssd_w_in, ssd_conv_w, ssd_conv_b, ssd_dt_bias, ssd_a_log, ssd_d_skip,
              ssd_gate_norm, ssd_w_out):
    yp, ys = x_prompt, x_sample
    kp_l, vp_l, hp_l, cp_l = [], [], [], []
    ks_l, vs_l, hs_l, cs_l = [], [], [], []
    for i in range(DEPTH):
        j = i // 2
        ffn_a = (ffn_norm[i, 0], ffn_w_gate_up[i, 0], ffn_w_down[i, 0])
        ffn_b = (ffn_norm[i, 1], ffn_w_gate_up[i, 1], ffn_w_down[i, 1])
        yp = yp + 0.5 * _swiglu_ffn(yp, *ffn_a)
        ys = ys + 0.5 * _swiglu_ffn(ys, *ffn_a)
        if i % 2 == 0:
            ap = (attn_norm[j], attn_w_qkv[j], attn_q_norm[j], attn_k_norm[j], attn_rel_bias[j], attn_w_o[j])
            o, k_new, v_new = _attn_prompt(yp, *ap)
            yp = yp + o
            kp_l.append(k_new)
            vp_l.append(v_new)
            o, k_new, v_new = _attn_sample(ys, cache_k[j], cache_v[j], *ap)
            ys = ys + o
            ks_l.append(k_new)
            vs_l.append(v_new)
        else:
            sp = (ssd_norm[j], ssd_w_in[j], ssd_conv_w[j], ssd_conv_b[j], ssd_dt_bias[j],
                  ssd_a_log[j], ssd_d_skip[j], ssd_gate_norm[j], ssd_w_out[j])
            h0 = jnp.zeros((yp.shape[0], N_SSM_HEADS, SSM_HEADDIM, D_STATE), yp.dtype)
            c0 = jnp.zeros((yp.shape[0], CONV_WIDTH - 1, CONV_DIM), yp.dtype)
            o, h_new, c_new = _ssd_mixer(yp, h0, c0, *sp)
            yp = yp + o
            hp_l.append(h_new)
            cp_l.append(c_new)
            o, h_new, c_new = _ssd_mixer(ys, state_ssm[j], state_conv[j], *sp)
            ys = ys + o
            hs_l.append(h_new)
            cs_l.append(c_new)
        yp = yp + 0.5 * _swiglu_ffn(yp, *ffn_b)
        ys = ys + 0.5 * _swiglu_ffn(ys, *ffn_b)
    return (yp, ys,
            jnp.stack(kp_l), jnp.stack(vp_l), jnp.stack(hp_l), jnp.stack(cp_l),
            jnp.stack(ks_l), jnp.stack(vs_l), jnp.stack(hs_l), jnp.stack(cs_l))
```

```python
import functools
import math

import jax
import jax.numpy as jnp
import numpy as np
from jax import lax
from jax.experimental import pallas as pl
from jax.experimental.pallas import tpu as pltpu

F32 = jnp.float32
BF16 = jnp.bfloat16

D_MODEL = 2048
CHUNK = 64
LEFT_CHUNKS = 8
HEAD_DIM = 128
N_HEADS = D_MODEL // HEAD_DIM
REL_CLIP = 128
PAST_LEN = 1024
D_INNER = 2 * D_MODEL
SSM_HEADDIM = 64
N_SSM_HEADS = D_INNER // SSM_HEADDIM
N_GROUPS = 8
D_STATE = 128
CONV_WIDTH = 4
CONV_DIM = D_INNER + 2 * N_GROUPS * D_STATE
EPS = 1e-6
NEG_INF = -1e30
LOG2E = math.log2(math.e)

LANES = 128
SUBLANES = 8
MXU_DIM = 256
MIB = 1024 * 1024

Q_BLOCK = 4 * CHUNK
KV_BLOCKS = LEFT_CHUNKS * CHUNK // Q_BLOCK + 1
SCAN_CHUNK = 128
PAIR = 2 * SSM_HEADDIM
N_PAIRS = N_SSM_HEADS // 2
PAIRS_PER_GROUP = N_PAIRS // N_GROUPS
COL_TILE = 1024
assert SCAN_CHUNK == D_STATE == PAIR == LANES

NT_DIMS = (((1,), (1,)), ((), ()))


def _params(semantics, vmem_mib):
    return pltpu.CompilerParams(dimension_semantics=semantics, vmem_limit_bytes=vmem_mib * MIB)


def _rms_scale(x):
    return lax.rsqrt(jnp.mean(x * x, axis=-1, keepdims=True) + EPS)


def _silu(x):
    return x * jax.nn.sigmoid(x)


def _norm_into(x_ref, g_ref, xn_ref, first):
    @pl.when(first)
    def _():
        x = x_ref[...]
        xn_ref[...] = (x * _rms_scale(x) * g_ref[...]).astype(BF16)


def _ffn_cast_kernel(x_ref, g_ref, wg_ref, wu_ref, wd_ref, o_ref, wgb_ref, wub_ref, wdb_ref, xn_ref):
    @pl.when(pl.program_id(0) == 0)
    def _():
        x = x_ref[...]
        xn_ref[...] = (x * _rms_scale(x) * g_ref[...]).astype(BF16)
        o_ref[...] = x

    wg = wg_ref[...].astype(BF16)
    wu = wu_ref[...].astype(BF16)
    wd = wd_ref[...].astype(BF16)
    wgb_ref[...] = wg
    wub_ref[...] = wu
    wdb_ref[...] = wd
    xn = xn_ref[...]
    gate = jnp.dot(xn, wg, preferred_element_type=F32)
    up = jnp.dot(xn, wu, preferred_element_type=F32)
    act = (_silu(gate) * up).astype(BF16)
    o_ref[...] += 0.5 * jnp.dot(act, wd, preferred_element_type=F32)


def _ffn_cast(x, g, w_gu, w_d, layer, half, *, tf):
    m, d = x.shape
    d_ff = w_d.shape[2]
    nf = d_ff // tf
    return pl.pallas_call(
        _ffn_cast_kernel,
        grid=(nf,),
        in_specs=[
            pl.BlockSpec((m, d), lambda f: (0, 0)),
            pl.BlockSpec((1, d), lambda f: (0, 0)),
            pl.BlockSpec((None, None, d, tf), lambda f: (layer, half, 0, f)),
            pl.BlockSpec((None, None, d, tf), lambda f: (layer, half, 0, f + nf)),
            pl.BlockSpec((None, None, tf, d), lambda f: (layer, half, f, 0)),
        ],
        out_specs=[
            pl.BlockSpec((m, d), lambda f: (0, 0)),
            pl.BlockSpec((d, tf), lambda f: (0, f)),
            pl.BlockSpec((d, tf), lambda f: (0, f)),
            pl.BlockSpec((tf, d), lambda f: (f, 0)),
        ],
        out_shape=[
            jax.ShapeDtypeStruct((m, d), F32),
            jax.ShapeDtypeStruct((d, d_ff), BF16),
            jax.ShapeDtypeStruct((d, d_ff), BF16),
            jax.ShapeDtypeStruct((d_ff, d), BF16),
        ],
        scratch_shapes=[pltpu.VMEM((m, d), BF16)],
        compiler_params=_params(("arbitrary",), 56),
        name="ffn_cast",
    )(x, g.reshape(1, d), w_gu, w_gu, w_d)


def _ffn_kernel(x_ref, g_ref, wg_ref, wu_ref, wd_ref, o_ref, xn_ref, *, sub):
    @pl.when(pl.program_id(1) == 0)
    def _():
        x = x_ref[...]
        xn_ref[...] = (x * _rms_scale(x) * g_ref[...]).astype(BF16)
        o_ref[...] = x

    xn = xn_ref[...]
    down = None
    for c in range(wg_ref.shape[1] // sub):
        cs = slice(c * sub, (c + 1) * sub)
        gate = jnp.dot(xn, wg_ref[:, cs], preferred_element_type=F32)
        up = jnp.dot(xn, wu_ref[:, cs], preferred_element_type=F32)
        act = (_silu(gate) * up).astype(BF16)
        part = jnp.dot(act, wd_ref[cs, :], preferred_element_type=F32)
        down = part if down is None else down + part
    o_ref[...] += 0.5 * down


def _ffn(x, g, w_g, w_u, w_d, *, tm, tf, sub):
    m, d = x.shape
    d_ff = w_d.shape[0]
    return pl.pallas_call(
        functools.partial(_ffn_kernel, sub=sub),
        grid=(m // tm, d_ff // tf),
        in_specs=[
            pl.BlockSpec((tm, d), lambda i, f: (i, 0)),
            pl.BlockSpec((1, d), lambda i, f: (0, 0)),
            pl.BlockSpec((d, tf), lambda i, f: (0, f)),
            pl.BlockSpec((d, tf), lambda i, f: (0, f)),
            pl.BlockSpec((tf, d), lambda i, f: (f, 0)),
        ],
        out_specs=pl.BlockSpec((tm, d), lambda i, f: (i, 0)),
        out_shape=jax.ShapeDtypeStruct((m, d), F32),
        scratch_shapes=[pltpu.VMEM((tm, d), BF16)],
        compiler_params=_params(("parallel", "arbitrary"), 58),
        name="ffn",
    )(x, g.reshape(1, d), w_g, w_u, w_d)


def _qkv_kernel(x_ref, g_ref, wqk_ref, wv_ref, gain_ref, qk_ref, v_ref, xn_ref, *, sub):
    n = pl.program_id(1)
    _norm_into(x_ref, g_ref, xn_ref, n == 0)
    xn = xn_ref[...]
    gain = gain_ref[pl.ds(n // (pl.num_programs(1) // 2), 1), :]
    for c in range(wqk_ref.shape[1] // sub):
        acc = jnp.dot(xn, wqk_ref[:, c * sub:(c + 1) * sub], preferred_element_type=F32)
        for h in range(sub // HEAD_DIM):
            a = acc[:, h * HEAD_DIM:(h + 1) * HEAD_DIM]
            cols = slice(c * sub + h * HEAD_DIM, c * sub + (h + 1) * HEAD_DIM)
            qk_ref[:, cols] = (a * _rms_scale(a) * gain).astype(qk_ref.dtype)
    v_ref[...] = jnp.dot(xn, wv_ref[...], preferred_element_type=F32).astype(v_ref.dtype)


def _qkv(x, g, w_qkv, layer, gains, *, tm, out_dtype):
    m, d = x.shape
    n_tiles = 2 * D_MODEL // COL_TILE
    tv = D_MODEL // n_tiles
    v0 = 2 * D_MODEL // tv
    return pl.pallas_call(
        functools.partial(_qkv_kernel, sub=MXU_DIM),
        grid=(m // tm, n_tiles),
        in_specs=[
            pl.BlockSpec((tm, d), lambda i, n: (i, 0)),
            pl.BlockSpec((1, d), lambda i, n: (0, 0)),
            pl.BlockSpec((None, d, COL_TILE), lambda i, n: (layer, 0, n)),
            pl.BlockSpec((None, d, tv), lambda i, n: (layer, 0, v0 + n)),
            pl.BlockSpec((2, HEAD_DIM), lambda i, n: (0, 0)),
        ],
        out_specs=[
            pl.BlockSpec((tm, COL_TILE), lambda i, n: (i, n)),
            pl.BlockSpec((tm, tv), lambda i, n: (i, n)),
        ],
        out_shape=[
            jax.ShapeDtypeStruct((m, 2 * D_MODEL), out_dtype),
            jax.ShapeDtypeStruct((m, D_MODEL), out_dtype),
        ],
        scratch_shapes=[pltpu.VMEM((tm, d), BF16)],
        compiler_params=_params(("parallel", "arbitrary"), 48),
        name="qkv",
    )(x, g.reshape(1, d), w_qkv, w_qkv, gains)


def _plain_proj_kernel(x_ref, g_ref, w_ref, o_ref, xn_ref):
    _norm_into(x_ref, g_ref, xn_ref, pl.program_id(1) == 0)
    o_ref[...] = jnp.dot(xn_ref[...], w_ref[...], preferred_element_type=F32).astype(o_ref.dtype)


def _ssd_z(x, g, w_in, layer, *, tm):
    m, d = x.shape
    return pl.pallas_call(
        _plain_proj_kernel,
        grid=(m // tm, D_INNER // COL_TILE),
        in_specs=[
            pl.BlockSpec((tm, d), lambda i, j: (i, 0)),
            pl.BlockSpec((1, d), lambda i, j: (0, 0)),
            pl.BlockSpec((None, d, COL_TILE), lambda i, j: (layer, 0, j)),
        ],
        out_specs=pl.BlockSpec((tm, COL_TILE), lambda i, j: (i, j)),
        out_shape=jax.ShapeDtypeStruct((m, D_INNER), BF16),
        scratch_shapes=[pltpu.VMEM((tm, d), BF16)],
        compiler_params=_params(("parallel", "arbitrary"), 48),
        name="ssd_z",
    )(x, g.reshape(1, d), w_in)


def _dt_kernel(x_ref, g_ref, w_ref, b_ref, o_ref):
    x = x_ref[...]
    xn = (x * _rms_scale(x) * g_ref[...]).astype(BF16)
    raw = jnp.dot(xn, w_ref[...], preferred_element_type=F32) + b_ref[...]
    o_ref[...] = jnp.maximum(raw, 0.0) + jnp.log1p(jnp.exp(-jnp.abs(raw)))


def _ssd_dt(x, g, w_dt, dt_bias, *, tm):
    m, d = x.shape
    return pl.pallas_call(
        _dt_kernel,
        grid=(m // tm,),
        in_specs=[
            pl.BlockSpec((tm, d), lambda i: (i, 0)),
            pl.BlockSpec((1, d), lambda i: (0, 0)),
            pl.BlockSpec((d, LANES), lambda i: (0, 0)),
            pl.BlockSpec((1, LANES), lambda i: (0, 0)),
        ],
        out_specs=pl.BlockSpec((tm, LANES), lambda i: (i, 0)),
        out_shape=jax.ShapeDtypeStruct((m, LANES), F32),
        compiler_params=_params(("parallel",), 32),
        name="ssd_dt",
    )(x, g.reshape(1, d), w_dt, dt_bias)


def _xbc_kernel(x_ref, g_ref, w_ref, cw_ref, cb_ref, init_ref, o_ref, st_ref, xn_ref, carry_ref, *, tm, sub,
                row_block):
    i = pl.program_id(1)
    n = pl.program_id(2)
    _norm_into(x_ref, g_ref, xn_ref, n == 0)

    @pl.when(i == 0)
    def _():
        carry_ref[n] = init_ref[...]

    rb = min(tm, row_block)
    for c in range(w_ref.shape[1] // sub):
        cs = slice(c * sub, (c + 1) * sub)
        taps = [cw_ref[t:t + 1, cs] for t in range(CONV_WIDTH)]
        carry = carry_ref[n, :, cs]
        for r in range(tm // rb):
            rows = slice(r * rb, (r + 1) * rb)
            raw = jnp.dot(xn_ref[rows, :], w_ref[:, cs], preferred_element_type=F32)
            conv = cb_ref[:, cs] + raw * taps[CONV_WIDTH - 1]
            for t in range(CONV_WIDTH - 1):
                conv = conv + pltpu.roll(raw, CONV_WIDTH - 1 - t, 0) * taps[t]
            o_ref[rows, cs] = _silu(conv).astype(o_ref.dtype)
            head = jnp.concatenate([carry, raw[0:SUBLANES]], axis=0)
            top = cb_ref[:, cs] + head[SUBLANES:] * taps[CONV_WIDTH - 1]
            for t in range(CONV_WIDTH - 1):
                lag = CONV_WIDTH - 1 - t
                top = top + head[SUBLANES - lag:2 * SUBLANES - lag] * taps[t]
            o_ref[r * rb:r * rb + SUBLANES, cs] = _silu(top).astype(o_ref.dtype)
            carry = raw[rb - SUBLANES:rb]
        carry_ref[n, :, cs] = carry
        st_ref[:, cs] = carry


def _ssd_xbc(x, g, w_in, layer, conv_w, conv_b, init, *, tm):
    b, l, d = x.shape
    n_tiles = CONV_DIM // COL_TILE
    col0 = D_INNER // COL_TILE
    sub = MXU_DIM
    return pl.pallas_call(
        functools.partial(_xbc_kernel, tm=tm, sub=sub, row_block=512),
        grid=(b, l // tm, n_tiles),
        in_specs=[
            pl.BlockSpec((None, tm, d), lambda bb, i, n: (bb, i, 0)),
            pl.BlockSpec((1, d), lambda bb, i, n: (0, 0)),
            pl.BlockSpec((None, d, COL_TILE), lambda bb, i, n: (layer, 0, col0 + n)),
            pl.BlockSpec((CONV_WIDTH, COL_TILE), lambda bb, i, n: (0, n)),
            pl.BlockSpec((1, COL_TILE), lambda bb, i, n: (0, n)),
            pl.BlockSpec((None, SUBLANES, COL_TILE), lambda bb, i, n: (bb, 0, n)),
        ],
        out_specs=[
            pl.BlockSpec((None, tm, COL_TILE), lambda bb, i, n: (bb, i, n)),
            pl.BlockSpec((None, None, SUBLANES, COL_TILE), lambda bb, i, n: (bb, i, 0, n)),
        ],
        out_shape=[
            jax.ShapeDtypeStruct((b, l, CONV_DIM), BF16),
            jax.ShapeDtypeStruct((b, l // tm, SUBLANES, CONV_DIM), F32),
        ],
        scratch_shapes=[
            pltpu.VMEM((tm, d), BF16),
            pltpu.VMEM((n_tiles, SUBLANES, COL_TILE), F32),
        ],
        compiler_params=_params(("arbitrary", "arbitrary", "arbitrary"), 48),
        name="ssd_xbc",
    )(x, g.reshape(1, d), w_in, conv_w, conv_b.reshape(1, CONV_DIM), init)


def _proj_res_kernel(a_ref, w_ref, x_ref, o_ref):
    o_ref[...] = x_ref[...] + jnp.dot(a_ref[...], w_ref[...], preferred_element_type=F32)


def _proj_res(a, w, layer, x, *, tm):
    m, k = a.shape
    n = w.shape[2]
    return pl.pallas_call(
        _proj_res_kernel,
        grid=(n // COL_TILE, m // tm),
        in_specs=[
            pl.BlockSpec((tm, k), lambda j, i: (i, 0)),
            pl.BlockSpec((None, k, COL_TILE), lambda j, i: (layer, 0, j)),
            pl.BlockSpec((tm, COL_TILE), lambda j, i: (i, j)),
        ],
        out_specs=pl.BlockSpec((tm, COL_TILE), lambda j, i: (i, j)),
        out_shape=jax.ShapeDtypeStruct((m, n), F32),
        compiler_params=_params(("parallel", "parallel"), 48),
        name="proj_res",
    )(a, w, x)


def _band_attn_kernel(q_ref, k0_ref, k1_ref, k2_ref, v0_ref, v1_ref, v2_ref, bias_ref, o_ref):
    k_refs = (k0_ref, k1_ref, k2_ref)
    v_refs = (v0_ref, v1_ref, v2_ref)
    for h in range(N_HEADS):
        cols = slice(h * HEAD_DIM, (h + 1) * HEAD_DIM)
        qh = q_ref[:, cols]
        scores = [lax.dot_general(qh, k_refs[t][:, cols], NT_DIMS, preferred_element_type=F32)
                  + bias_ref[h, :, t * Q_BLOCK:(t + 1) * Q_BLOCK] for t in range(KV_BLOCKS)]
        row_max = jnp.max(functools.reduce(jnp.maximum, scores), axis=-1, keepdims=True)
        probs = [jnp.exp2(s - row_max) for s in scores]
        denom = jnp.sum(functools.reduce(jnp.add, probs), axis=-1, keepdims=True)
        out = functools.reduce(jnp.add, [
            jnp.dot(p.astype(BF16), v_refs[t][:, cols], preferred_element_type=F32)
            for t, p in enumerate(probs)])
        o_ref[:, cols] = (out / denom).astype(o_ref.dtype)


def _band_attn(qk, v, bias):
    b, l, _ = v.shape

    def kv_spec(col, t):
        return pl.BlockSpec((None, Q_BLOCK, D_MODEL),
                            lambda bb, j: (bb, jnp.maximum(j - (KV_BLOCKS - 1 - t), 0), col))

    return pl.pallas_call(
        _band_attn_kernel,
        grid=(b, l // Q_BLOCK),
        in_specs=[pl.BlockSpec((None, Q_BLOCK, D_MODEL), lambda bb, j: (bb, j, 0))]
        + [kv_spec(1, t) for t in range(KV_BLOCKS)]
        + [kv_spec(0, t) for t in range(KV_BLOCKS)]
        + [pl.BlockSpec((None,) + bias.shape[1:], lambda bb, j: (jnp.minimum(j, KV_BLOCKS - 1), 0, 0, 0),
                        pipeline_mode=pl.Buffered(1))],
        out_specs=pl.BlockSpec((None, Q_BLOCK, D_MODEL), lambda bb, j: (bb, j, 0)),
        out_shape=jax.ShapeDtypeStruct((b, l, D_MODEL), BF16),
        compiler_params=_params(("parallel", "parallel"), 48),
        name="band_attn",
    )(qk, qk, qk, qk, v, v, v, bias)


def _step_attn_kernel(qk_ref, v_ref, kc_ref, vc_ref, bias_c_ref, bias_n_ref, o_ref):
    for h in range(N_HEADS):
        cols = slice(h * HEAD_DIM, (h + 1) * HEAD_DIM)
        qh = qk_ref[:, cols].astype(BF16)
        kn = qk_ref[:, D_MODEL + h * HEAD_DIM:D_MODEL + (h + 1) * HEAD_DIM].astype(BF16)
        vn = v_ref[:, cols].astype(BF16)
        kc = kc_ref[:, cols].astype(BF16)
        vc = vc_ref[:, cols].astype(BF16)
        s_c = lax.dot_general(qh, kc, NT_DIMS, preferred_element_type=F32) + bias_c_ref[h]
        s_n = lax.dot_general(qh, kn, NT_DIMS, preferred_element_type=F32) + bias_n_ref[h]
        row_max = jnp.maximum(jnp.max(s_c, axis=-1, keepdims=True), jnp.max(s_n, axis=-1, keepdims=True))
        p_c = jnp.exp2(s_c - row_max)
        p_n = jnp.exp2(s_n - row_max)
        denom = jnp.sum(p_c, axis=-1, keepdims=True) + jnp.sum(p_n, axis=-1, keepdims=True)
        out = (jnp.dot(p_c.astype(BF16), vc, preferred_element_type=F32)
               + jnp.dot(p_n.astype(BF16), vn, preferred_element_type=F32))
        o_ref[:, cols] = (out / denom).astype(o_ref.dtype)


def _step_attn(qk, v, k_cache, v_cache, bias_c, bias_n):
    b, s, _ = v.shape
    w = k_cache.shape[1]
    return pl.pallas_call(
        _step_attn_kernel,
        grid=(b,),
        in_specs=[
            pl.BlockSpec((None, s, 2 * D_MODEL), lambda i: (i, 0, 0)),
            pl.BlockSpec((None, s, D_MODEL), lambda i: (i, 0, 0)),
            pl.BlockSpec((None, w, D_MODEL), lambda i: (i, 0, 0)),
            pl.BlockSpec((None, w, D_MODEL), lambda i: (i, 0, 0)),
            pl.BlockSpec(bias_c.shape, lambda i: (0, 0, 0)),
            pl.BlockSpec(bias_n.shape, lambda i: (0, 0, 0)),
        ],
        out_specs=pl.BlockSpec((None, s, D_MODEL), lambda i: (i, 0, 0)),
        out_shape=jax.ShapeDtypeStruct((b, s, D_MODEL), BF16),
        compiler_params=_params(("parallel",), 48),
        name="step_attn",
    )(qk, v, k_cache, v_cache, bias_c, bias_n)


def _band_bias(table):
    width = KV_BLOCKS * Q_BLOCK
    period = width + Q_BLOCK
    m = np.arange(period)
    lag = np.where(m < width, m, m - period)
    dist = (KV_BLOCKS - 1) * Q_BLOCK - lag
    idx = np.clip(dist, -REL_CLIP, REL_CLIP) + REL_CLIP
    u = table[:, idx] * LOG2E
    skew = jnp.tile(u, (1, Q_BLOCK))[:, :Q_BLOCK * (period - 1)].reshape(-1, Q_BLOCK, period - 1)[:, :, :width]
    qi = np.arange(Q_BLOCK)[:, None]
    kn = np.arange(width)[None, :]
    chunk_gap = (qi // CHUNK + (KV_BLOCKS - 1) * (Q_BLOCK // CHUNK)) - kn // CHUNK
    in_band = (chunk_gap >= 0) & (chunk_gap <= LEFT_CHUNKS)
    variant = np.arange(KV_BLOCKS)[:, None, None]
    visible = in_band[None] & (kn[None] // Q_BLOCK >= KV_BLOCKS - 1 - variant)
    return jnp.where(visible[:, None], skew[None], NEG_INF).astype(F32)


def _step_bias(table, s, w):
    q_pos = PAST_LEN + np.arange(s)
    k_pos = np.concatenate([PAST_LEN - w + np.arange(w), q_pos])
    qch, kch = q_pos // CHUNK, k_pos // CHUNK
    mask = (kch[None, :] <= qch[:, None]) & (kch[None, :] >= qch[:, None] - LEFT_CHUNKS)
    idx = np.clip(q_pos[:, None] - k_pos[None, :], -REL_CLIP, REL_CLIP) + REL_CLIP
    bias = jnp.where(mask[None], table[:, idx] * LOG2E, NEG_INF).astype(F32)
    return bias[:, :, :w], bias[:, :, w:]


def _ssd_scan_kernel(xs_ref, b_ref, c_ref, dt_ref, z_ref, h0_ref, a_ref, dskip_ref, gg_ref,
                     yn_ref, hout_ref, state_ref, y_ref, acs_t_ref, dt_t_ref, w_t_ref, *, rows):
    lc = SCAN_CHUNK
    i = pl.program_id(1)

    @pl.when(i == 0)
    def _():
        state_ref[...] = h0_ref[...]

    row_id = lax.broadcasted_iota(jnp.int32, (lc, lc), 0)
    col_id = lax.broadcasted_iota(jnp.int32, (lc, lc), 1)
    causal = col_id <= row_id
    causal_f = causal.astype(F32)
    low_half = lax.broadcasted_iota(jnp.int32, (lc, PAIR), 1) < SSM_HEADDIM
    low_half_row = lax.broadcasted_iota(jnp.int32, (1, PAIR), 1) < SSM_HEADDIM

    def chunk(c, carry):
        rows_c = pl.ds(pl.multiple_of(c * lc, lc), lc)
        dt = dt_ref[rows_c, :]
        a_cs = jnp.dot(causal_f, dt * a_ref[...], precision=lax.Precision.HIGHEST, preferred_element_type=F32)
        a_cs_t = a_cs.T
        dt_t = dt.T
        acs_t_ref[...] = a_cs_t
        dt_t_ref[...] = dt_t
        w_t_ref[...] = dt_t * jnp.exp(a_cs_t[:, lc - 1:lc] - a_cs_t)
        chunk_decay = jnp.exp(a_cs[lc - 1:lc, :])
        for g in range(N_GROUPS):
            gcols = slice(g * D_STATE, (g + 1) * D_STATE)
            bg = b_ref[rows_c, gcols]
            cg = c_ref[rows_c, gcols]
            cb = lax.dot_general(cg, bg, NT_DIMS, preferred_element_type=F32)
            bg_t = bg.astype(F32).T
            cg_f = cg.astype(F32)
            for jp in range(PAIRS_PER_GROUP):
                q = g * PAIRS_PER_GROUP + jp
                pcols = slice(q * PAIR, (q + 1) * PAIR)
                xp = xs_ref[rows_c, pcols]
                h_t = state_ref[q]
                lhs_y, lhs_s = [], []
                for h in (2 * q, 2 * q + 1):
                    col = jnp.broadcast_to(a_cs[:, h:h + 1], (lc, lc))
                    row = jnp.broadcast_to(acs_t_ref[h:h + 1, :], (lc, lc))
                    seg = jnp.exp(jnp.where(causal, col - row, -jnp.inf))
                    within = cb * seg * jnp.broadcast_to(dt_t_ref[h:h + 1, :], (lc, lc))
                    carried = cg_f * jnp.exp(col)
                    lhs_y.append(jnp.concatenate([within.astype(BF16), carried.astype(BF16)], axis=1))
                    lhs_s.append((bg_t * jnp.broadcast_to(w_t_ref[h:h + 1, :], (D_STATE, lc))).astype(BF16))
                y2 = jnp.dot(jnp.concatenate(lhs_y, axis=0), jnp.concatenate([xp, h_t.astype(BF16)], axis=0),
                             preferred_element_type=F32)
                s2 = jnp.dot(jnp.concatenate(lhs_s, axis=0), xp, preferred_element_type=F32)
                decay = jnp.where(low_half_row,
                                  jnp.broadcast_to(chunk_decay[:, 2 * q:2 * q + 1], (1, PAIR)),
                                  jnp.broadcast_to(chunk_decay[:, 2 * q + 1:2 * q + 2], (1, PAIR)))
                state_ref[q] = h_t * decay + jnp.where(low_half, s2[:D_STATE], s2[D_STATE:])
                y_ref[rows_c, pcols] = (jnp.where(low_half, y2[:lc], y2[lc:])
                                        + dskip_ref[:, pcols] * xp.astype(F32))
        return carry

    lax.fori_loop(0, rows // lc, chunk, 0)

    y = y_ref[...] * _silu(z_ref[...].astype(F32))
    yn_ref[...] = (y * _rms_scale(y) * gg_ref[...]).astype(yn_ref.dtype)

    @pl.when(i == pl.num_programs(1) - 1)
    def _():
        hout_ref[...] = state_ref[...]


def _ssd_scan(xbc, dt, z, h0, a_neg, d_skip, gate_g, *, rows):
    b, l, _ = xbc.shape
    gn = N_GROUPS * D_STATE
    b_block = D_INNER // gn
    return pl.pallas_call(
        functools.partial(_ssd_scan_kernel, rows=rows),
        grid=(b, l // rows),
        in_specs=[
            pl.BlockSpec((None, rows, D_INNER), lambda bb, i: (bb, i, 0)),
            pl.BlockSpec((None, rows, gn), lambda bb, i: (bb, i, b_block)),
            pl.BlockSpec((None, rows, gn), lambda bb, i: (bb, i, b_block + 1)),
            pl.BlockSpec((None, rows, LANES), lambda bb, i: (bb, i, 0)),
            pl.BlockSpec((None, rows, D_INNER), lambda bb, i: (bb, i, 0)),
            pl.BlockSpec((None, N_PAIRS, D_STATE, PAIR), lambda bb, i: (bb, 0, 0, 0)),
            pl.BlockSpec((1, LANES), lambda bb, i: (0, 0)),
            pl.BlockSpec((1, D_INNER), lambda bb, i: (0, 0)),
            pl.BlockSpec((1, D_INNER), lambda bb, i: (0, 0)),
        ],
        out_specs=[
            pl.BlockSpec((None, rows, D_INNER), lambda bb, i: (bb, i, 0)),
            pl.BlockSpec((None, N_PAIRS, D_STATE, PAIR), lambda bb, i: (bb, 0, 0, 0)),
        ],
        out_shape=[
            jax.ShapeDtypeStruct((b, l, D_INNER), BF16),
            jax.ShapeDtypeStruct((b, N_PAIRS, D_STATE, PAIR), F32),
        ],
        scratch_shapes=[
            pltpu.VMEM((N_PAIRS, D_STATE, PAIR), F32),
            pltpu.VMEM((rows, D_INNER), F32),
            pltpu.VMEM((LANES, SCAN_CHUNK), F32),
            pltpu.VMEM((LANES, SCAN_CHUNK), F32),
            pltpu.VMEM((LANES, SCAN_CHUNK), F32),
        ],
        compiler_params=_params(("arbitrary", "arbitrary"), 48),
        name="ssd_scan",
    )(xbc, xbc, xbc, dt, z, h0, a_neg, d_skip, gate_g)


def _to_pair_layout(h):
    b = h.shape[0]
    return h.reshape(b, N_PAIRS, 2, SSM_HEADDIM, D_STATE).transpose(0, 1, 4, 2, 3).reshape(b, N_PAIRS, D_STATE, PAIR)


def _from_pair_layout(h):
    b = h.shape[0]
    return (h.reshape(b, N_PAIRS, D_STATE, 2, SSM_HEADDIM).transpose(0, 1, 3, 4, 2)
            .reshape(b, N_SSM_HEADS, SSM_HEADDIM, D_STATE))


def _ffn_layer(xp, xs, g, w_gu, w_d, layer, half):
    xs, w_g, w_u, w_dn = _ffn_cast(xs, g, w_gu, w_d, layer, half, tf=256)
    xp = _ffn(xp, g, w_g, w_u, w_dn, tm=1024, tf=512, sub=MXU_DIM)
    return xp, xs


def _attn_layer(xp, xs, cache_k, cache_v, g, w_qkv, w_o, layer, q_gain, k_gain, table, *, batch, seq):
    dec_batch, w = cache_k.shape[0], cache_k.shape[1]
    dec_seq = xs.shape[0] // dec_batch
    gains = jnp.stack([q_gain * (HEAD_DIM ** -0.5 * LOG2E), k_gain]).astype(F32)
    qk, v = _qkv(xp, g, w_qkv, layer, gains, tm=1024, out_dtype=BF16)
    win = min(LEFT_CHUNKS * CHUNK, seq)
    x_tail = xp.reshape(batch, seq, D_MODEL)[:, seq - win:].reshape(batch * win, D_MODEL)
    qk_tail, v_tail = _qkv(x_tail, g, w_qkv, layer, gains, tm=batch * win, out_dtype=F32)
    k_new = qk_tail[:, D_MODEL:].reshape(batch, win, N_HEADS, HEAD_DIM)
    v_new = v_tail.reshape(batch, win, N_HEADS, HEAD_DIM)
    o = _band_attn(qk.reshape(batch, seq, 2 * D_MODEL), v.reshape(batch, seq, D_MODEL), _band_bias(table))
    xp = _proj_res(o.reshape(batch * seq, D_MODEL), w_o, layer, xp, tm=512)
    m_s = dec_batch * dec_seq
    qk_s, v_s = _qkv(xs, g, w_qkv, layer, gains, tm=m_s, out_dtype=F32)
    ks_new = qk_s[:, D_MODEL:].reshape(dec_batch, dec_seq, N_HEADS, HEAD_DIM)
    vs_new = v_s.reshape(dec_batch, dec_seq, N_HEADS, HEAD_DIM)
    bias_c, bias_n = _step_bias(table, dec_seq, w)
    o_s = _step_attn(qk_s.reshape(dec_batch, dec_seq, 2 * D_MODEL), v_s.reshape(dec_batch, dec_seq, D_MODEL),
                     cache_k.reshape(dec_batch, w, D_MODEL), cache_v.reshape(dec_batch, w, D_MODEL),
                     bias_c, bias_n)
    xs = _proj_res(o_s.reshape(m_s, D_MODEL), w_o, layer, xs, tm=m_s)
    return xp, xs, k_new, v_new, ks_new, vs_new


def _ssd_stream(x2d, h0_pairs, conv_init, p, *, batch, seq, tm, rows):
    g, w_in, w_out, layer, w_dt, conv_w, conv_b, dt_bias, a_neg, d_skip, gate_g = p
    row_tile = min(1024, batch * seq)
    z = _ssd_z(x2d, g, w_in, layer, tm=row_tile)
    xbc, tails = _ssd_xbc(x2d.reshape(batch, seq, D_MODEL), g, w_in, layer, conv_w, conv_b, conv_init, tm=tm)
    dt = _ssd_dt(x2d, g, w_dt, dt_bias, tm=row_tile)
    z = z.reshape(batch, seq, D_INNER)
    dt = dt.reshape(batch, seq, LANES)
    pad = (-seq) % SCAN_CHUNK
    if pad:
        widen = lambda t: jnp.pad(t, ((0, 0), (0, pad), (0, 0)))
        xbc, z, dt = widen(xbc), widen(z), widen(dt)
    yn, h_new = _ssd_scan(xbc, dt, z, h0_pairs, a_neg, d_skip, gate_g, rows=rows)
    yn = yn[:, :seq].reshape(batch * seq, D_INNER)
    x2d = _proj_res(yn, w_out, layer, x2d, tm=min(512, batch * seq))
    return x2d, _from_pair_layout(h_new), tails[:, -1, SUBLANES - (CONV_WIDTH - 1):]


def _ssd_layer(xp, xs, state_ssm, state_conv, g, w_in, w_out, layer, w_dt_f32, conv_w, conv_b, dt_bias, a_log,
               d_skip, gate_g, *, batch, seq):
    dec_batch = state_ssm.shape[0]
    dec_seq = xs.shape[0] // dec_batch
    lane_pad = (0, LANES - N_SSM_HEADS)
    p = (g, w_in, w_out, layer,
         jnp.pad(w_dt_f32, ((0, 0), lane_pad)).astype(BF16), conv_w, conv_b,
         jnp.pad(dt_bias.astype(F32), lane_pad).reshape(1, LANES),
         jnp.pad(-jnp.exp(a_log.astype(F32)), lane_pad).reshape(1, LANES),
         jnp.repeat(d_skip, SSM_HEADDIM).reshape(1, D_INNER),
         gate_g.reshape(1, D_INNER))
    zero_state = jnp.zeros((batch, N_PAIRS, D_STATE, PAIR), F32)
    zero_conv = jnp.zeros((batch, SUBLANES, CONV_DIM), F32)
    xp, hp, cp = _ssd_stream(xp, zero_state, zero_conv, p, batch=batch, seq=seq, tm=1024, rows=2 * SCAN_CHUNK)
    conv_init = jnp.pad(state_conv, ((0, 0), (SUBLANES - (CONV_WIDTH - 1), 0), (0, 0)))
    xs, hs, cs = _ssd_stream(xs, _to_pair_layout(state_ssm), conv_init, p, batch=dec_batch, seq=dec_seq,
                             tm=dec_seq, rows=SCAN_CHUNK)
    return xp, xs, hp, cp, hs, cs


def kernel(x_prompt, x_sample, cache_k, cache_v, state_ssm, state_conv, ffn_norm, ffn_w_gate_up, ffn_w_down, attn_norm, attn_w_qkv, attn_q_norm, attn_k_norm, attn_rel_bias, attn_w_o, ssd_norm, ssd_w_in, ssd_conv_w, ssd_conv_b, ssd_dt_bias, ssd_a_log, ssd_d_skip, ssd_gate_norm, ssd_w_out):
    batch, seq, _ = x_prompt.shape
    dec_batch, dec_seq, _ = x_sample.shape
    depth = ffn_norm.shape[0]
    xp = x_prompt.reshape(batch * seq, D_MODEL)
    xs = x_sample.reshape(dec_batch * dec_seq, D_MODEL)

    w_qkv = attn_w_qkv.astype(BF16)
    w_o = attn_w_o.astype(BF16)
    w_in = ssd_w_in.astype(BF16)
    w_out = ssd_w_out.astype(BF16)

    kp, vp, hp, cp, ks, vs, hs, cs = [], [], [], [], [], [], [], []
    for i in range(depth):
        j = i // 2
        xp, xs = _ffn_layer(xp, xs, ffn_norm[i, 0], ffn_w_gate_up, ffn_w_down, i, 0)
        if i % 2 == 0:
            xp, xs, k_new, v_new, ks_new, vs_new = _attn_layer(
                xp, xs, cache_k[j], cache_v[j], attn_norm[j], w_qkv, w_o, j, attn_q_norm[j], attn_k_norm[j],
                attn_rel_bias[j], batch=batch, seq=seq)
            kp.append(k_new); vp.append(v_new); ks.append(ks_new); vs.append(vs_new)
        else:
            xp, xs, h_p, c_p, h_s, c_s = _ssd_layer(
                xp, xs, state_ssm[j], state_conv[j], ssd_norm[j], w_in, w_out, j,
                ssd_w_in[j][:, D_INNER + CONV_DIM:], ssd_conv_w[j], ssd_conv_b[j], ssd_dt_bias[j], ssd_a_log[j],
                ssd_d_skip[j], ssd_gate_norm[j], batch=batch, seq=seq)
            hp.append(h_p); cp.append(c_p); hs.append(h_s); cs.append(c_s)
        xp, xs = _ffn_layer(xp, xs, ffn_norm[i, 1], ffn_w_gate_up, ffn_w_down, i, 1)

    return (xp.reshape(batch, seq, D_MODEL), xs.reshape(dec_batch, dec_seq, D_MODEL),
            jnp.stack(kp), jnp.stack(vp), jnp.stack(hp), jnp.stack(cp),
            jnp.stack(ks), jnp.stack(vs), jnp.stack(hs), jnp.stack(cs))
```

```python
import functools
import math

import jax
import jax.numpy as jnp
import numpy as np
from jax import lax
from jax.experimental import pallas as pl
from jax.experimental.pallas import tpu as pltpu

F32 = jnp.float32
BF16 = jnp.bfloat16

D_MODEL = 2048
CHUNK = 64
LEFT_CHUNKS = 8
HEAD_DIM = 128
N_HEADS = D_MODEL // HEAD_DIM
REL_CLIP = 128
PAST_LEN = 1024
D_INNER = 2 * D_MODEL
SSM_HEADDIM = 64
N_SSM_HEADS = D_INNER // SSM_HEADDIM
N_GROUPS = 8
D_STATE = 128
CONV_WIDTH = 4
CONV_DIM = D_INNER + 2 * N_GROUPS * D_STATE
EPS = 1e-6
NEG_INF = -1e30
LOG2E = math.log2(math.e)

LANES = 128
SUBLANES = 8
MXU_DIM = 256
MIB = 1024 * 1024

Q_BLOCK = 4 * CHUNK
KV_BLOCKS = LEFT_CHUNKS * CHUNK // Q_BLOCK + 1
SCAN_CHUNK = 128
PAIR = 2 * SSM_HEADDIM
N_PAIRS = N_SSM_HEADS // 2
PAIRS_PER_GROUP = N_PAIRS // N_GROUPS
COL_TILE = 1024
assert SCAN_CHUNK == D_STATE == PAIR == LANES

NT_DIMS = (((1,), (1,)), ((), ()))


def _params(semantics, vmem_mib):
    return pltpu.CompilerParams(dimension_semantics=semantics, vmem_limit_bytes=vmem_mib * MIB)


def _rms_scale(x):
    return lax.rsqrt(jnp.mean(x * x, axis=-1, keepdims=True) + EPS)


def _silu(x):
    return x * jax.nn.sigmoid(x)


def _norm_into(x_ref, g_ref, xn_ref, first):
    @pl.when(first)
    def _():
        x = x_ref[...]
        xn_ref[...] = (x * _rms_scale(x) * g_ref[...]).astype(BF16)


def _ffn_cast_kernel(x_ref, g_ref, wg_ref, wu_ref, wd_ref, o_ref, wgb_ref, wub_ref, wdb_ref, xn_ref):
    @pl.when(pl.program_id(0) == 0)
    def _():
        x = x_ref[...]
        xn_ref[...] = (x * _rms_scale(x) * g_ref[...]).astype(BF16)
        o_ref[...] = x

    wg = wg_ref[...].astype(BF16)
    wu = wu_ref[...].astype(BF16)
    wd = wd_ref[...].astype(BF16)
    wgb_ref[...] = wg
    wub_ref[...] = wu
    wdb_ref[...] = wd
    xn = xn_ref[...]
    gate = jnp.dot(xn, wg, preferred_element_type=F32)
    up = jnp.dot(xn, wu, preferred_element_type=F32)
    act = (_silu(gate) * up).astype(BF16)
    o_ref[...] += 0.5 * jnp.dot(act, wd, preferred_element_type=F32)


def _ffn_cast(x, g, w_gu, w_d, layer, half, *, tf):
    m, d = x.shape
    d_ff = w_d.shape[2]
    nf = d_ff // tf
    return pl.pallas_call(
        _ffn_cast_kernel,
        grid=(nf,),
        in_specs=[
            pl.BlockSpec((m, d), lambda f: (0, 0)),
            pl.BlockSpec((1, d), lambda f: (0, 0)),
            pl.BlockSpec((None, None, d, tf), lambda f: (layer, half, 0, f)),
            pl.BlockSpec((None, None, d, tf), lambda f: (layer, half, 0, f + nf)),
            pl.BlockSpec((None, None, tf, d), lambda f: (layer, half, f, 0)),
        ],
        out_specs=[
            pl.BlockSpec((m, d), lambda f: (0, 0)),
            pl.BlockSpec((d, tf), lambda f: (0, f)),
            pl.BlockSpec((d, tf), lambda f: (0, f)),
            pl.BlockSpec((tf, d), lambda f: (f, 0)),
        ],
        out_shape=[
            jax.ShapeDtypeStruct((m, d), F32),
            jax.ShapeDtypeStruct((d, d_ff), BF16),
            jax.ShapeDtypeStruct((d, d_ff), BF16),
            jax.ShapeDtypeStruct((d_ff, d), BF16),
        ],
        scratch_shapes=[pltpu.VMEM((m, d), BF16)],
        compiler_params=_params(("arbitrary",), 56),
        name="ffn_cast",
    )(x, g.reshape(1, d), w_gu, w_gu, w_d)


def _ffn_kernel(x_ref, g_ref, wg_ref, wu_ref, wd_ref, o_ref, xn_ref, *, sub):
    @pl.when(pl.program_id(1) == 0)
    def _():
        x = x_ref[...]
        xn_ref[...] = (x * _rms_scale(x) * g_ref[...]).astype(BF16)
        o_ref[...] = x

    xn = xn_ref[...]
    down = None
    for c in range(wg_ref.shape[1] // sub):
        cs = slice(c * sub, (c + 1) * sub)
        gate = jnp.dot(xn, wg_ref[:, cs], preferred_element_type=F32)
        up = jnp.dot(xn, wu_ref[:, cs], preferred_element_type=F32)
        act = (_silu(gate) * up).astype(BF16)
        part = jnp.dot(act, wd_ref[cs, :], preferred_element_type=F32)
        down = part if down is None else down + part
    o_ref[...] += 0.5 * down


def _ffn(x, g, w_g, w_u, w_d, *, tm, tf, sub):
    m, d = x.shape
    d_ff = w_d.shape[0]
    return pl.pallas_call(
        functools.partial(_ffn_kernel, sub=sub),
        grid=(m // tm, d_ff // tf),
        in_specs=[
            pl.BlockSpec((tm, d), lambda i, f: (i, 0)),
            pl.BlockSpec((1, d), lambda i, f: (0, 0)),
            pl.BlockSpec((d, tf), lambda i, f: (0, f)),
            pl.BlockSpec((d, tf), lambda i, f: (0, f)),
            pl.BlockSpec((tf, d), lambda i, f: (f, 0)),
        ],
        out_specs=pl.BlockSpec((tm, d), lambda i, f: (i, 0)),
        out_shape=jax.ShapeDtypeStruct((m, d), F32),
        scratch_shapes=[pltpu.VMEM((tm, d), BF16)],
        compiler_params=_params(("parallel", "arbitrary"), 58),
        name="ffn",
    )(x, g.reshape(1, d), w_g, w_u, w_d)


def _qkv_kernel(x_ref, g_ref, wqk_ref, wv_ref, gain_ref, qk_ref, v_ref, xn_ref, *, sub):
    n = pl.program_id(1)
    _norm_into(x_ref, g_ref, xn_ref, n == 0)
    xn = xn_ref[...]
    gain = gain_ref[pl.ds(n // (pl.num_programs(1) // 2), 1), :]
    for c in range(wqk_ref.shape[1] // sub):
        acc = jnp.dot(xn, wqk_ref[:, c * sub:(c + 1) * sub], preferred_element_type=F32)
        for h in range(sub // HEAD_DIM):
            a = acc[:, h * HEAD_DIM:(h + 1) * HEAD_DIM]
            cols = slice(c * sub + h * HEAD_DIM, c * sub + (h + 1) * HEAD_DIM)
            qk_ref[:, cols] = (a * _rms_scale(a) * gain).astype(qk_ref.dtype)
    v_ref[...] = jnp.dot(xn, wv_ref[...], preferred_element_type=F32).astype(v_ref.dtype)


def _qkv(x, g, w_qkv, layer, gains, *, tm, out_dtype):
    m, d = x.shape
    n_tiles = 2 * D_MODEL // COL_TILE
    tv = D_MODEL // n_tiles
    v0 = 2 * D_MODEL // tv
    return pl.pallas_call(
        functools.partial(_qkv_kernel, sub=MXU_DIM),
        grid=(m // tm, n_tiles),
        in_specs=[
            pl.BlockSpec((tm, d), lambda i, n: (i, 0)),
            pl.BlockSpec((1, d), lambda i, n: (0, 0)),
            pl.BlockSpec((None, d, COL_TILE), lambda i, n: (layer, 0, n)),
            pl.BlockSpec((None, d, tv), lambda i, n: (layer, 0, v0 + n)),
            pl.BlockSpec((2, HEAD_DIM), lambda i, n: (0, 0)),
        ],
        out_specs=[
            pl.BlockSpec((tm, COL_TILE), lambda i, n: (i, n)),
            pl.BlockSpec((tm, tv), lambda i, n: (i, n)),
        ],
        out_shape=[
            jax.ShapeDtypeStruct((m, 2 * D_MODEL), out_dtype),
            jax.ShapeDtypeStruct((m, D_MODEL), out_dtype),
        ],
        scratch_shapes=[pltpu.VMEM((tm, d), BF16)],
        compiler_params=_params(("parallel", "arbitrary"), 48),
        name="qkv",
    )(x, g.reshape(1, d), w_qkv, w_qkv, gains)


def _z_dt_kernel(x_ref, g_ref, w_ref, wdt_ref, b_ref, z_ref, dt_ref, xn_ref):
    @pl.when(pl.program_id(1) == 0)
    def _():
        x = x_ref[...]
        xn = (x * _rms_scale(x) * g_ref[...]).astype(BF16)
        xn_ref[...] = xn
        raw = jnp.dot(xn, wdt_ref[...], preferred_element_type=F32) + b_ref[...]
        dt_ref[...] = jnp.maximum(raw, 0.0) + jnp.log1p(jnp.exp(-jnp.abs(raw)))

    z_ref[...] = jnp.dot(xn_ref[...], w_ref[...], preferred_element_type=F32).astype(z_ref.dtype)


def _ssd_z_dt(x, g, w_in, layer, w_dt, dt_bias, *, tm):
    m, d = x.shape
    return pl.pallas_call(
        _z_dt_kernel,
        grid=(m // tm, D_INNER // COL_TILE),
        in_specs=[
            pl.BlockSpec((tm, d), lambda i, j: (i, 0)),
            pl.BlockSpec((1, d), lambda i, j: (0, 0)),
            pl.BlockSpec((None, d, COL_TILE), lambda i, j: (layer, 0, j)),
            pl.BlockSpec((None, d, LANES), lambda i, j: (layer, 0, 0)),
            pl.BlockSpec((1, LANES), lambda i, j: (0, 0)),
        ],
        out_specs=[
            pl.BlockSpec((tm, COL_TILE), lambda i, j: (i, j)),
            pl.BlockSpec((tm, LANES), lambda i, j: (i, 0)),
        ],
        out_shape=[
            jax.ShapeDtypeStruct((m, D_INNER), BF16),
            jax.ShapeDtypeStruct((m, LANES), F32),
        ],
        scratch_shapes=[pltpu.VMEM((tm, d), BF16)],
        compiler_params=_params(("parallel", "arbitrary"), 48),
        name="ssd_z_dt",
    )(x, g.reshape(1, d), w_in, w_dt, dt_bias)


def _xbc_kernel(x_ref, g_ref, w_ref, cw_ref, cb_ref, init_ref, o_ref, st_ref, xn_ref, carry_ref, *, tm, sub,
                row_block, streams):
    i = pl.program_id(1)
    n = pl.program_id(2)
    _norm_into(x_ref, g_ref, xn_ref, n == 0)
    chained = streams == 1

    if chained:
        @pl.when(i == 0)
        def _():
            carry_ref[n] = init_ref[0]

    rb = min(tm, row_block) if chained else tm // streams
    for c in range(w_ref.shape[1] // sub):
        cs = slice(c * sub, (c + 1) * sub)
        taps = [cw_ref[t:t + 1, cs] for t in range(CONV_WIDTH)]
        carry = carry_ref[n, :, cs] if chained else None
        for r in range(tm // rb):
            rows = slice(r * rb, (r + 1) * rb)
            if not chained:
                carry = init_ref[r, :, cs]
            raw = jnp.dot(xn_ref[rows, :], w_ref[:, cs], preferred_element_type=F32)
            conv = cb_ref[:, cs] + raw * taps[CONV_WIDTH - 1]
            for t in range(CONV_WIDTH - 1):
                conv = conv + pltpu.roll(raw, CONV_WIDTH - 1 - t, 0) * taps[t]
            o_ref[rows, cs] = _silu(conv).astype(o_ref.dtype)
            head = jnp.concatenate([carry, raw[0:SUBLANES]], axis=0)
            top = cb_ref[:, cs] + head[SUBLANES:] * taps[CONV_WIDTH - 1]
            for t in range(CONV_WIDTH - 1):
                lag = CONV_WIDTH - 1 - t
                top = top + head[SUBLANES - lag:2 * SUBLANES - lag] * taps[t]
            o_ref[r * rb:r * rb + SUBLANES, cs] = _silu(top).astype(o_ref.dtype)
            carry = raw[rb - SUBLANES:rb]
            if not chained:
                st_ref[r, :, cs] = carry
        if chained:
            carry_ref[n, :, cs] = carry
            st_ref[0, :, cs] = carry


def _ssd_xbc(x, g, w_in, layer, conv_w, conv_b, init, *, tm):
    b, l, d = x.shape
    streams = init.shape[1]
    n_tiles = CONV_DIM // COL_TILE
    col0 = D_INNER // COL_TILE
    assert streams == 1 or l == tm
    return pl.pallas_call(
        functools.partial(_xbc_kernel, tm=tm, sub=MXU_DIM, row_block=512, streams=streams),
        grid=(b, l // tm, n_tiles),
        in_specs=[
            pl.BlockSpec((None, tm, d), lambda bb, i, n: (bb, i, 0)),
            pl.BlockSpec((1, d), lambda bb, i, n: (0, 0)),
            pl.BlockSpec((None, d, COL_TILE), lambda bb, i, n: (layer, 0, col0 + n)),
            pl.BlockSpec((CONV_WIDTH, COL_TILE), lambda bb, i, n: (0, n)),
            pl.BlockSpec((1, COL_TILE), lambda bb, i, n: (0, n)),
            pl.BlockSpec((None, streams, SUBLANES, COL_TILE), lambda bb, i, n: (bb, 0, 0, n)),
        ],
        out_specs=[
            pl.BlockSpec((None, tm, COL_TILE), lambda bb, i, n: (bb, i, n)),
            pl.BlockSpec((None, None, streams, SUBLANES, COL_TILE), lambda bb, i, n: (bb, i, 0, 0, n)),
        ],
        out_shape=[
            jax.ShapeDtypeStruct((b, l, CONV_DIM), BF16),
            jax.ShapeDtypeStruct((b, l // tm, streams, SUBLANES, CONV_DIM), F32),
        ],
        scratch_shapes=[
            pltpu.VMEM((tm, d), BF16),
            pltpu.VMEM((n_tiles, SUBLANES, COL_TILE), F32),
        ],
        compiler_params=_params(("arbitrary", "arbitrary", "arbitrary"), 48),
        name="ssd_xbc",
    )(x, g.reshape(1, d), w_in, conv_w, conv_b.reshape(1, CONV_DIM), init)


def _proj_res_kernel(a_ref, w_ref, x_ref, o_ref):
    o_ref[...] = x_ref[...] + jnp.dot(a_ref[...], w_ref[...], preferred_element_type=F32)


def _proj_res(a, w, layer, x, *, tm):
    m, k = a.shape
    n = w.shape[2]
    return pl.pallas_call(
        _proj_res_kernel,
        grid=(n // COL_TILE, m // tm),
        in_specs=[
            pl.BlockSpec((tm, k), lambda j, i: (i, 0)),
            pl.BlockSpec((None, k, COL_TILE), lambda j, i: (layer, 0, j)),
            pl.BlockSpec((tm, COL_TILE), lambda j, i: (i, j)),
        ],
        out_specs=pl.BlockSpec((tm, COL_TILE), lambda j, i: (i, j)),
        out_shape=jax.ShapeDtypeStruct((m, n), F32),
        compiler_params=_params(("parallel", "parallel"), 56),
        name="proj_res",
    )(a, w, x)


def _band_attn_kernel(q_ref, k0_ref, k1_ref, k2_ref, v0_ref, v1_ref, v2_ref, bias_ref, o_ref):
    k_refs = (k0_ref, k1_ref, k2_ref)
    v_refs = (v0_ref, v1_ref, v2_ref)
    for h in range(N_HEADS):
        cols = slice(h * HEAD_DIM, (h + 1) * HEAD_DIM)
        qh = q_ref[:, cols]
        scores = [lax.dot_general(qh, k_refs[t][:, cols], NT_DIMS, preferred_element_type=F32)
                  + bias_ref[h, :, t * Q_BLOCK:(t + 1) * Q_BLOCK] for t in range(KV_BLOCKS)]
        row_max = jnp.max(functools.reduce(jnp.maximum, scores), axis=-1, keepdims=True)
        probs = [jnp.exp2(s - row_max) for s in scores]
        denom = jnp.sum(functools.reduce(jnp.add, probs), axis=-1, keepdims=True)
        out = functools.reduce(jnp.add, [
            jnp.dot(p.astype(BF16), v_refs[t][:, cols], preferred_element_type=F32)
            for t, p in enumerate(probs)])
        o_ref[:, cols] = (out / denom).astype(o_ref.dtype)


def _band_attn(qk, v, bias):
    b, l, _ = v.shape

    def kv_spec(col, t):
        return pl.BlockSpec((None, Q_BLOCK, D_MODEL),
                            lambda bb, j: (bb, jnp.maximum(j - (KV_BLOCKS - 1 - t), 0), col))

    return pl.pallas_call(
        _band_attn_kernel,
        grid=(b, l // Q_BLOCK),
        in_specs=[pl.BlockSpec((None, Q_BLOCK, D_MODEL), lambda bb, j: (bb, j, 0))]
        + [kv_spec(1, t) for t in range(KV_BLOCKS)]
        + [kv_spec(0, t) for t in range(KV_BLOCKS)]
        + [pl.BlockSpec((None,) + bias.shape[1:], lambda bb, j: (jnp.minimum(j, KV_BLOCKS - 1), 0, 0, 0),
                        pipeline_mode=pl.Buffered(1))],
        out_specs=pl.BlockSpec((None, Q_BLOCK, D_MODEL), lambda bb, j: (bb, j, 0)),
        out_shape=jax.ShapeDtypeStruct((b, l, D_MODEL), BF16),
        compiler_params=_params(("parallel", "parallel"), 48),
        name="band_attn",
    )(qk, qk, qk, qk, v, v, v, bias)


def _step_attn_kernel(qk_ref, v_ref, kc_ref, vc_ref, bias_c_ref, bias_n_ref, o_ref):
    for h in range(N_HEADS):
        cols = slice(h * HEAD_DIM, (h + 1) * HEAD_DIM)
        qh = qk_ref[:, cols].astype(BF16)
        kn = qk_ref[:, D_MODEL + h * HEAD_DIM:D_MODEL + (h + 1) * HEAD_DIM].astype(BF16)
        vn = v_ref[:, cols].astype(BF16)
        kc = kc_ref[:, h, :].astype(BF16)
        vc = vc_ref[:, h, :].astype(BF16)
        s_c = lax.dot_general(qh, kc, NT_DIMS, preferred_element_type=F32) + bias_c_ref[h]
        s_n = lax.dot_general(qh, kn, NT_DIMS, preferred_element_type=F32) + bias_n_ref[h]
        row_max = jnp.maximum(jnp.max(s_c, axis=-1, keepdims=True), jnp.max(s_n, axis=-1, keepdims=True))
        p_c = jnp.exp2(s_c - row_max)
        p_n = jnp.exp2(s_n - row_max)
        denom = jnp.sum(p_c, axis=-1, keepdims=True) + jnp.sum(p_n, axis=-1, keepdims=True)
        out = (jnp.dot(p_c.astype(BF16), vc, preferred_element_type=F32)
               + jnp.dot(p_n.astype(BF16), vn, preferred_element_type=F32))
        o_ref[:, cols] = (out / denom).astype(o_ref.dtype)


def _step_attn(qk, v, k_cache, v_cache, layer, bias_c, bias_n):
    b, s, _ = v.shape
    w = k_cache.shape[2]
    cache_spec = pl.BlockSpec((None, None, w, N_HEADS, HEAD_DIM), lambda i: (layer, i, 0, 0, 0))
    return pl.pallas_call(
        _step_attn_kernel,
        grid=(b,),
        in_specs=[
            pl.BlockSpec((None, s, 2 * D_MODEL), lambda i: (i, 0, 0)),
            pl.BlockSpec((None, s, D_MODEL), lambda i: (i, 0, 0)),
            cache_spec,
            cache_spec,
            pl.BlockSpec(bias_c.shape, lambda i: (0, 0, 0)),
            pl.BlockSpec(bias_n.shape, lambda i: (0, 0, 0)),
        ],
        out_specs=pl.BlockSpec((None, s, D_MODEL), lambda i: (i, 0, 0)),
        out_shape=jax.ShapeDtypeStruct((b, s, D_MODEL), BF16),
        compiler_params=_params(("parallel",), 48),
        name="step_attn",
    )(qk, v, k_cache, v_cache, bias_c, bias_n)


def _skewed_bias(table, rows, width, dist0):
    period = width + rows
    m = np.arange(period)
    lag = np.where(m < width, m, m - period)
    idx = np.clip(dist0 - lag, -REL_CLIP, REL_CLIP) + REL_CLIP
    u = table[:, idx] * LOG2E
    return jnp.tile(u, (1, rows))[:, :rows * (period - 1)].reshape(-1, rows, period - 1)[:, :, :width]


def _band_bias(table):
    width = KV_BLOCKS * Q_BLOCK
    skew = _skewed_bias(table, Q_BLOCK, width, (KV_BLOCKS - 1) * Q_BLOCK)
    qi = np.arange(Q_BLOCK)[:, None]
    kn = np.arange(width)[None, :]
    chunk_gap = (qi // CHUNK + (KV_BLOCKS - 1) * (Q_BLOCK // CHUNK)) - kn // CHUNK
    in_band = (chunk_gap >= 0) & (chunk_gap <= LEFT_CHUNKS)
    variant = np.arange(KV_BLOCKS)[:, None, None]
    visible = in_band[None] & (kn[None] // Q_BLOCK >= KV_BLOCKS - 1 - variant)
    return jnp.where(visible[:, None], skew[None], NEG_INF).astype(F32)


def _step_bias(table, s, w):
    q_pos = PAST_LEN + np.arange(s)
    k_pos = np.concatenate([PAST_LEN - w + np.arange(w), q_pos])
    qch, kch = q_pos // CHUNK, k_pos // CHUNK
    mask = (kch[None, :] <= qch[:, None]) & (kch[None, :] >= qch[:, None] - LEFT_CHUNKS)
    bias_c = jnp.where(mask[None, :, :w], _skewed_bias(table, s, w, w), NEG_INF).astype(F32)
    bias_n = jnp.where(mask[None, :, w:], _skewed_bias(table, s, s, 0), NEG_INF).astype(F32)
    return bias_c, bias_n


def _ssd_scan_kernel(xs_ref, b_ref, c_ref, dt_ref, z_ref, h0_ref, a_ref, dskip_ref, gg_ref,
                     yn_ref, hout_ref, state_ref, y_ref, acs_t_ref, dt_t_ref, w_t_ref, *, rows):
    lc = SCAN_CHUNK
    i = pl.program_id(1)

    @pl.when(i == 0)
    def _():
        state_ref[...] = h0_ref[...]

    row_id = lax.broadcasted_iota(jnp.int32, (lc, lc), 0)
    col_id = lax.broadcasted_iota(jnp.int32, (lc, lc), 1)
    causal = col_id <= row_id
    causal_f = causal.astype(F32)
    low_half = lax.broadcasted_iota(jnp.int32, (lc, PAIR), 1) < SSM_HEADDIM
    low_half_row = lax.broadcasted_iota(jnp.int32, (1, PAIR), 1) < SSM_HEADDIM

    def chunk(c, carry):
        rows_c = pl.ds(pl.multiple_of(c * lc, lc), lc)
        dt = dt_ref[rows_c, :]
        a_cs = jnp.dot(causal_f, dt * a_ref[...], precision=lax.Precision.HIGHEST, preferred_element_type=F32)
        a_cs_t = a_cs.T
        dt_t = dt.T
        acs_t_ref[...] = a_cs_t
        dt_t_ref[...] = dt_t
        w_t_ref[...] = dt_t * jnp.exp(a_cs_t[:, lc - 1:lc] - a_cs_t)
        chunk_decay = jnp.exp(a_cs[lc - 1:lc, :])
        for g in range(N_GROUPS):
            gcols = slice(g * D_STATE, (g + 1) * D_STATE)
            bg = b_ref[rows_c, gcols]
            cg = c_ref[rows_c, gcols]
            cb = lax.dot_general(cg, bg, NT_DIMS, preferred_element_type=F32)
            bg_t = bg.astype(F32).T
            cg_f = cg.astype(F32)
            for jp in range(PAIRS_PER_GROUP):
                q = g * PAIRS_PER_GROUP + jp
                pcols = slice(q * PAIR, (q + 1) * PAIR)
                xp = xs_ref[rows_c, pcols]
                h_t = state_ref[q]
                lhs_y, lhs_s = [], []
                for h in (2 * q, 2 * q + 1):
                    col = jnp.broadcast_to(a_cs[:, h:h + 1], (lc, lc))
                    row = jnp.broadcast_to(acs_t_ref[h:h + 1, :], (lc, lc))
                    seg = jnp.exp(jnp.where(causal, col - row, -jnp.inf))
                    within = cb * seg * jnp.broadcast_to(dt_t_ref[h:h + 1, :], (lc, lc))
                    carried = cg_f * jnp.exp(col)
                    lhs_y.append(jnp.concatenate([within.astype(BF16), carried.astype(BF16)], axis=1))
                    lhs_s.append((bg_t * jnp.broadcast_to(w_t_ref[h:h + 1, :], (D_STATE, lc))).astype(BF16))
                y2 = jnp.dot(jnp.concatenate(lhs_y, axis=0), jnp.concatenate([xp, h_t.astype(BF16)], axis=0),
                             preferred_element_type=F32)
                s2 = jnp.dot(jnp.concatenate(lhs_s, axis=0), xp, preferred_element_type=F32)
                decay = jnp.where(low_half_row,
                                  jnp.broadcast_to(chunk_decay[:, 2 * q:2 * q + 1], (1, PAIR)),
                                  jnp.broadcast_to(chunk_decay[:, 2 * q + 1:2 * q + 2], (1, PAIR)))
                state_ref[q] = h_t * decay + jnp.where(low_half, s2[:D_STATE], s2[D_STATE:])
                y_ref[rows_c, pcols] = (jnp.where(low_half, y2[:lc], y2[lc:])
                                        + dskip_ref[:, pcols] * xp.astype(F32))
        return carry

    lax.fori_loop(0, rows // lc, chunk, 0)

    y = y_ref[...] * _silu(z_ref[...].astype(F32))
    yn_ref[...] = (y * _rms_scale(y) * gg_ref[...]).astype(yn_ref.dtype)

    @pl.when(i == pl.num_programs(1) - 1)
    def _():
        hout_ref[...] = state_ref[...]


def _ssd_scan(xbc, dt, z, h0, a_neg, d_skip, gate_g, *, rows):
    b, l, _ = xbc.shape
    gn = N_GROUPS * D_STATE
    b_block = D_INNER // gn
    return pl.pallas_call(
        functools.partial(_ssd_scan_kernel, rows=rows),
        grid=(b, l // rows),
        in_specs=[
            pl.BlockSpec((None, rows, D_INNER), lambda bb, i: (bb, i, 0)),
            pl.BlockSpec((None, rows, gn), lambda bb, i: (bb, i, b_block)),
            pl.BlockSpec((None, rows, gn), lambda bb, i: (bb, i, b_block + 1)),
            pl.BlockSpec((None, rows, LANES), lambda bb, i: (bb, i, 0)),
            pl.BlockSpec((None, rows, D_INNER), lambda bb, i: (bb, i, 0)),
            pl.BlockSpec((None, N_PAIRS, D_STATE, PAIR), lambda bb, i: (bb, 0, 0, 0)),
            pl.BlockSpec((1, LANES), lambda bb, i: (0, 0)),
            pl.BlockSpec((1, D_INNER), lambda bb, i: (0, 0)),
            pl.BlockSpec((1, D_INNER), lambda bb, i: (0, 0)),
        ],
        out_specs=[
            pl.BlockSpec((None, rows, D_INNER), lambda bb, i: (bb, i, 0)),
            pl.BlockSpec((None, N_PAIRS, D_STATE, PAIR), lambda bb, i: (bb, 0, 0, 0)),
        ],
        out_shape=[
            jax.ShapeDtypeStruct((b, l, D_INNER), BF16),
            jax.ShapeDtypeStruct((b, N_PAIRS, D_STATE, PAIR), F32),
        ],
        scratch_shapes=[
            pltpu.VMEM((N_PAIRS, D_STATE, PAIR), F32),
            pltpu.VMEM((rows, D_INNER), F32),
            pltpu.VMEM((LANES, SCAN_CHUNK), F32),
            pltpu.VMEM((LANES, SCAN_CHUNK), F32),
            pltpu.VMEM((LANES, SCAN_CHUNK), F32),
        ],
        compiler_params=_params(("arbitrary", "arbitrary"), 48),
        name="ssd_scan",
    )(xbc, xbc, xbc, dt, z, h0, a_neg, d_skip, gate_g)


def _to_pair_layout(h):
    b = h.shape[0]
    return h.reshape(b, N_PAIRS, 2, SSM_HEADDIM, D_STATE).transpose(0, 1, 4, 2, 3).reshape(b, N_PAIRS, D_STATE, PAIR)


def _from_pair_layout(h):
    b = h.shape[0]
    return (h.reshape(b, N_PAIRS, D_STATE, 2, SSM_HEADDIM).transpose(0, 1, 3, 4, 2)
            .reshape(b, N_SSM_HEADS, SSM_HEADDIM, D_STATE))


def _ffn_layer(xp, xs, g, w_gu, w_d, layer, half):
    xs, w_g, w_u, w_dn = _ffn_cast(xs, g, w_gu, w_d, layer, half, tf=256)
    xp = _ffn(xp, g, w_g, w_u, w_dn, tm=1024, tf=512, sub=MXU_DIM)
    return xp, xs


def _attn_layer(xp, xs, cache_k, cache_v, g, w_qkv, w_o, layer, q_gain, k_gain, table, *, batch, seq):
    dec_batch, w = cache_k.shape[1], cache_k.shape[2]
    dec_seq = xs.shape[0] // dec_batch
    gains = jnp.stack([q_gain * (HEAD_DIM ** -0.5 * LOG2E), k_gain]).astype(F32)
    qk, v = _qkv(xp, g, w_qkv, layer, gains, tm=1024, out_dtype=BF16)
    win = min(LEFT_CHUNKS * CHUNK, seq)
    x_tail = xp.reshape(batch, seq, D_MODEL)[:, seq - win:].reshape(batch * win, D_MODEL)
    qk_tail, v_tail = _qkv(x_tail, g, w_qkv, layer, gains, tm=batch * win, out_dtype=F32)
    k_new = qk_tail[:, D_MODEL:].reshape(batch, win, N_HEADS, HEAD_DIM)
    v_new = v_tail.reshape(batch, win, N_HEADS, HEAD_DIM)
    o = _band_attn(qk.reshape(batch, seq, 2 * D_MODEL), v.reshape(batch, seq, D_MODEL), _band_bias(table))
    xp = _proj_res(o.reshape(batch * seq, D_MODEL), w_o, layer, xp, tm=1024)
    m_s = dec_batch * dec_seq
    qk_s, v_s = _qkv(xs, g, w_qkv, layer, gains, tm=m_s, out_dtype=F32)
    ks_new = qk_s[:, D_MODEL:].reshape(dec_batch, dec_seq, N_HEADS, HEAD_DIM)
    vs_new = v_s.reshape(dec_batch, dec_seq, N_HEADS, HEAD_DIM)
    bias_c, bias_n = _step_bias(table, dec_seq, w)
    o_s = _step_attn(qk_s.reshape(dec_batch, dec_seq, 2 * D_MODEL), v_s.reshape(dec_batch, dec_seq, D_MODEL),
                     cache_k, cache_v, layer, bias_c, bias_n)
    xs = _proj_res(o_s.reshape(m_s, D_MODEL), w_o, layer, xs, tm=m_s)
    return xp, xs, k_new, v_new, ks_new, vs_new


def _ssd_stream(x2d, h0_pairs, conv_init, p, *, batch, seq, conv_batch, tm, rows):
    g, w_in, w_out, layer, w_dt, conv_w, conv_b, dt_bias, a_neg, d_skip, gate_g = p
    row_tile = min(1024, batch * seq)
    z, dt = _ssd_z_dt(x2d, g, w_in, layer, w_dt, dt_bias, tm=row_tile)
    xbc, tails = _ssd_xbc(x2d.reshape(conv_batch, -1, D_MODEL), g, w_in, layer, conv_w, conv_b, conv_init, tm=tm)
    xbc = xbc.reshape(batch, seq, CONV_DIM)
    z = z.reshape(batch, seq, D_INNER)
    dt = dt.reshape(batch, seq, LANES)
    pad = (-seq) % SCAN_CHUNK
    if pad:
        widen = lambda t: jnp.pad(t, ((0, 0), (0, pad), (0, 0)))
        xbc, z, dt = widen(xbc), widen(z), widen(dt)
    yn, h_new = _ssd_scan(xbc, dt, z, h0_pairs, a_neg, d_skip, gate_g, rows=rows)
    yn = yn[:, :seq].reshape(batch * seq, D_INNER)
    x2d = _proj_res(yn, w_out, layer, x2d, tm=row_tile)
    conv_state = tails[:, -1, :, SUBLANES - (CONV_WIDTH - 1):].reshape(batch, CONV_WIDTH - 1, CONV_DIM)
    return x2d, _from_pair_layout(h_new), conv_state


def _ssd_layer(xp, xs, state_ssm, state_conv, g, w_in, w_out, layer, w_dt, conv_w, conv_b, dt_bias, a_log,
               d_skip, gate_g, *, batch, seq):
    dec_batch = state_ssm.shape[0]
    dec_seq = xs.shape[0] // dec_batch
    lane_pad = (0, LANES - N_SSM_HEADS)
    p = (g, w_in, w_out, layer, w_dt, conv_w, conv_b,
         jnp.pad(dt_bias.astype(F32), lane_pad).reshape(1, LANES),
         jnp.pad(-jnp.exp(a_log.astype(F32)), lane_pad).reshape(1, LANES),
         jnp.repeat(d_skip, SSM_HEADDIM).reshape(1, D_INNER),
         gate_g.reshape(1, D_INNER))
    zero_state = jnp.zeros((batch, N_PAIRS, D_STATE, PAIR), F32)
    zero_conv = jnp.zeros((batch, 1, SUBLANES, CONV_DIM), F32)
    xp, hp, cp = _ssd_stream(xp, zero_state, zero_conv, p, batch=batch, seq=seq, conv_batch=batch, tm=1024,
                             rows=2 * SCAN_CHUNK)
    conv_init = jnp.pad(state_conv, ((0, 0), (SUBLANES - (CONV_WIDTH - 1), 0), (0, 0)))[None]
    xs, hs, cs = _ssd_stream(xs, _to_pair_layout(state_ssm), conv_init, p, batch=dec_batch, seq=dec_seq,
                             conv_batch=1, tm=dec_batch * dec_seq, rows=SCAN_CHUNK)
    return xp, xs, hp, cp, hs, cs


def kernel(x_prompt, x_sample, cache_k, cache_v, state_ssm, state_conv, ffn_norm, ffn_w_gate_up, ffn_w_down, attn_norm, attn_w_qkv, attn_q_norm, attn_k_norm, attn_rel_bias, attn_w_o, ssd_norm, ssd_w_in, ssd_conv_w, ssd_conv_b, ssd_dt_bias, ssd_a_log, ssd_d_skip, ssd_gate_norm, ssd_w_out):
    batch, seq, _ = x_prompt.shape
    dec_batch, dec_seq, _ = x_sample.shape
    depth = ffn_norm.shape[0]
    xp = x_prompt.reshape(batch * seq, D_MODEL)
    xs = x_sample.reshape(dec_batch * dec_seq, D_MODEL)

    w_qkv = attn_w_qkv.astype(BF16)
    w_o = attn_w_o.astype(BF16)
    w_in = ssd_w_in.astype(BF16)
    w_out = ssd_w_out.astype(BF16)
    w_dt = jnp.pad(w_in[:, :, D_INNER + CONV_DIM:], ((0, 0), (0, 0), (0, LANES - N_SSM_HEADS)))

    kp, vp, hp, cp, ks, vs, hs, cs = [], [], [], [], [], [], [], []
    for i in range(depth):
        j = i // 2
        xp, xs = _ffn_layer(xp, xs, ffn_norm[i, 0], ffn_w_gate_up, ffn_w_down, i, 0)
        if i % 2 == 0:
            xp, xs, k_new, v_new, ks_new, vs_new = _attn_layer(
                xp, xs, cache_k, cache_v, attn_norm[j], w_qkv, w_o, j, attn_q_norm[j], attn_k_norm[j],
                attn_rel_bias[j], batch=batch, seq=seq)
            kp.append(k_new); vp.append(v_new); ks.append(ks_new); vs.append(vs_new)
        else:
            xp, xs, h_p, c_p, h_s, c_s = _ssd_layer(
                xp, xs, state_ssm[j], state_conv[j], ssd_norm[j], w_in, w_out, j,
                w_dt, ssd_conv_w[j], ssd_conv_b[j], ssd_dt_bias[j], ssd_a_log[j],
                ssd_d_skip[j], ssd_gate_norm[j], batch=batch, seq=seq)
            hp.append(h_p); cp.append(c_p); hs.append(h_s); cs.append(c_s)
        xp, xs = _ffn_layer(xp, xs, ffn_norm[i, 1], ffn_w_gate_up, ffn_w_down, i, 1)

    return (xp.reshape(batch, seq, D_MODEL), xs.reshape(dec_batch, dec_seq, D_MODEL),
            jnp.stack(kp), jnp.stack(vp), jnp.stack(hp), jnp.stack(cp),
            jnp.stack(ks), jnp.stack(vs), jnp.stack(hs), jnp.stack(cs))
```

```python
import functools
import math

import jax
import jax.numpy as jnp
import numpy as np
from jax import lax
from jax.experimental import pallas as pl
from jax.experimental.pallas import tpu as pltpu

F32 = jnp.float32
BF16 = jnp.bfloat16

D_MODEL = 2048
CHUNK = 64
LEFT_CHUNKS = 8
HEAD_DIM = 128
N_HEADS = D_MODEL // HEAD_DIM
REL_CLIP = 128
PAST_LEN = 1024
D_INNER = 2 * D_MODEL
SSM_HEADDIM = 64
N_SSM_HEADS = D_INNER // SSM_HEADDIM
N_GROUPS = 8
D_STATE = 128
CONV_WIDTH = 4
CONV_DIM = D_INNER + 2 * N_GROUPS * D_STATE
EPS = 1e-6
NEG_INF = -1e30
LOG2E = math.log2(math.e)

LANES = 128
SUBLANES = 8
MXU_DIM = 256
MIB = 1024 * 1024

Q_BLOCK = 4 * CHUNK
KV_BLOCKS = LEFT_CHUNKS * CHUNK // Q_BLOCK + 1
SCAN_CHUNK = 128
PAIR = 2 * SSM_HEADDIM
N_PAIRS = N_SSM_HEADS // 2
PAIRS_PER_GROUP = N_PAIRS // N_GROUPS
COL_TILE = 1024
assert SCAN_CHUNK == D_STATE == PAIR == LANES

NT_DIMS = (((1,), (1,)), ((), ()))


def _params(semantics, vmem_mib):
    return pltpu.CompilerParams(dimension_semantics=semantics, vmem_limit_bytes=vmem_mib * MIB)


def _rms_scale(x):
    return lax.rsqrt(jnp.mean(x * x, axis=-1, keepdims=True) + EPS)


def _silu(x):
    return x * jax.nn.sigmoid(x)


def _with_norm_on_first(x_ref, g_ref, xn_ref, first, work):
    @pl.when(first)
    def _():
        x = x_ref[...]
        xn_ref[...] = (x * _rms_scale(x) * g_ref[...]).astype(BF16)
        work()

    @pl.when(jnp.logical_not(first))
    def _():
        work()


def _ffn_cast_kernel(x_ref, g_ref, wg_ref, wu_ref, wd_ref, o_ref, wgb_ref, wub_ref, wdb_ref, xn_ref):
    @pl.when(pl.program_id(0) == 0)
    def _():
        x = x_ref[...]
        xn_ref[...] = (x * _rms_scale(x) * g_ref[...]).astype(BF16)
        o_ref[...] = x

    wg = wg_ref[...].astype(BF16)
    wu = wu_ref[...].astype(BF16)
    wd = wd_ref[...].astype(BF16)
    wgb_ref[...] = wg
    wub_ref[...] = wu
    wdb_ref[...] = wd
    xn = xn_ref[...]
    gate = jnp.dot(xn, wg, preferred_element_type=F32)
    up = jnp.dot(xn, wu, preferred_element_type=F32)
    act = (_silu(gate) * up).astype(BF16)
    o_ref[...] += 0.5 * jnp.dot(act, wd, preferred_element_type=F32)


def _ffn_cast(x, g, w_gu, w_d, layer, half, *, tf):
    m, d = x.shape
    d_ff = w_d.shape[2]
    nf = d_ff // tf
    return pl.pallas_call(
        _ffn_cast_kernel,
        grid=(nf,),
        in_specs=[
            pl.BlockSpec((m, d), lambda f: (0, 0)),
            pl.BlockSpec((1, d), lambda f: (0, 0)),
            pl.BlockSpec((None, None, d, tf), lambda f: (layer, half, 0, f)),
            pl.BlockSpec((None, None, d, tf), lambda f: (layer, half, 0, f + nf)),
            pl.BlockSpec((None, None, tf, d), lambda f: (layer, half, f, 0)),
        ],
        out_specs=[
            pl.BlockSpec((m, d), lambda f: (0, 0)),
            pl.BlockSpec((d, tf), lambda f: (0, f)),
            pl.BlockSpec((d, tf), lambda f: (0, f)),
            pl.BlockSpec((tf, d), lambda f: (f, 0)),
        ],
        out_shape=[
            jax.ShapeDtypeStruct((m, d), F32),
            jax.ShapeDtypeStruct((d, d_ff), BF16),
            jax.ShapeDtypeStruct((d, d_ff), BF16),
            jax.ShapeDtypeStruct((d_ff, d), BF16),
        ],
        scratch_shapes=[pltpu.VMEM((m, d), BF16)],
        compiler_params=_params(("arbitrary",), 56),
        name="ffn_cast",
    )(x, g.reshape(1, d), w_gu, w_gu, w_d)


def _ffn_kernel(x_ref, g_ref, wg_ref, wu_ref, wd_ref, o_ref, xn_ref, *, sub):
    def half_down():
        xn = xn_ref[...]
        down = None
        for c in range(wg_ref.shape[1] // sub):
            cs = slice(c * sub, (c + 1) * sub)
            gate = jnp.dot(xn, wg_ref[:, cs], preferred_element_type=F32)
            up = jnp.dot(xn, wu_ref[:, cs], preferred_element_type=F32)
            act = (_silu(gate) * up).astype(BF16)
            part = jnp.dot(act, wd_ref[cs, :], preferred_element_type=F32)
            down = part if down is None else down + part
        return 0.5 * down

    first = pl.program_id(1) == 0

    @pl.when(first)
    def _():
        x = x_ref[...]
        xn_ref[...] = (x * _rms_scale(x) * g_ref[...]).astype(BF16)
        o_ref[...] = x_ref[...] + half_down()

    @pl.when(jnp.logical_not(first))
    def _():
        o_ref[...] += half_down()


def _ffn(x, g, w_g, w_u, w_d, *, tm, tf, sub):
    m, d = x.shape
    d_ff = w_d.shape[0]
    return pl.pallas_call(
        functools.partial(_ffn_kernel, sub=sub),
        grid=(m // tm, d_ff // tf),
        in_specs=[
            pl.BlockSpec((tm, d), lambda i, f: (i, 0)),
            pl.BlockSpec((1, d), lambda i, f: (0, 0)),
            pl.BlockSpec((d, tf), lambda i, f: (0, f)),
            pl.BlockSpec((d, tf), lambda i, f: (0, f)),
            pl.BlockSpec((tf, d), lambda i, f: (f, 0)),
        ],
        out_specs=pl.BlockSpec((tm, d), lambda i, f: (i, 0)),
        out_shape=jax.ShapeDtypeStruct((m, d), F32),
        scratch_shapes=[pltpu.VMEM((tm, d), BF16)],
        compiler_params=_params(("parallel", "arbitrary"), 58),
        name="ffn",
    )(x, g.reshape(1, d), w_g, w_u, w_d)


def _qkv_kernel(x_ref, g_ref, wqk_ref, wv_ref, gain_ref, qk_ref, v_ref, *rest, sub, emit_bf16):
    xn_ref = rest[-1]
    n = pl.program_id(1)

    def work():
        if emit_bf16:
            wqk, wv = rest[0], rest[1]
            wqk[...] = wqk_ref[...].astype(BF16)
            wv[...] = wv_ref[...].astype(BF16)
        else:
            wqk, wv = wqk_ref, wv_ref
        xn = xn_ref[...]
        gain = gain_ref[pl.ds(n // (pl.num_programs(1) // 2), 1), :]
        for c in range(wqk_ref.shape[1] // sub):
            acc = jnp.dot(xn, wqk[:, c * sub:(c + 1) * sub], preferred_element_type=F32)
            for h in range(sub // HEAD_DIM):
                a = acc[:, h * HEAD_DIM:(h + 1) * HEAD_DIM]
                cols = slice(c * sub + h * HEAD_DIM, c * sub + (h + 1) * HEAD_DIM)
                qk_ref[:, cols] = (a * _rms_scale(a) * gain).astype(qk_ref.dtype)
        v_ref[...] = jnp.dot(xn, wv[...], preferred_element_type=F32).astype(v_ref.dtype)

    _with_norm_on_first(x_ref, g_ref, xn_ref, n == 0, work)


def _qkv(x, g, w_qk, w_v, layer, v_col0, gains, *, tm, out_dtype, emit_bf16=False):
    m, d = x.shape
    n_tiles = 2 * D_MODEL // COL_TILE
    tv = D_MODEL // n_tiles
    v0 = v_col0 // tv
    assert not emit_bf16 or m == tm
    out_specs = [
        pl.BlockSpec((tm, COL_TILE), lambda i, n: (i, n)),
        pl.BlockSpec((tm, tv), lambda i, n: (i, n)),
    ]
    out_shape = [
        jax.ShapeDtypeStruct((m, 2 * D_MODEL), out_dtype),
        jax.ShapeDtypeStruct((m, D_MODEL), out_dtype),
    ]
    if emit_bf16:
        out_specs += [pl.BlockSpec((d, COL_TILE), lambda i, n: (0, n)), pl.BlockSpec((d, tv), lambda i, n: (0, n))]
        out_shape += [jax.ShapeDtypeStruct((d, 2 * D_MODEL), BF16), jax.ShapeDtypeStruct((d, D_MODEL), BF16)]
    return pl.pallas_call(
        functools.partial(_qkv_kernel, sub=MXU_DIM, emit_bf16=emit_bf16),
        grid=(m // tm, n_tiles),
        in_specs=[
            pl.BlockSpec((tm, d), lambda i, n: (i, 0)),
            pl.BlockSpec((1, d), lambda i, n: (0, 0)),
            pl.BlockSpec((None, d, COL_TILE), lambda i, n: (layer, 0, n)),
            pl.BlockSpec((None, d, tv), lambda i, n: (layer, 0, v0 + n)),
            pl.BlockSpec((2, HEAD_DIM), lambda i, n: (0, 0)),
        ],
        out_specs=out_specs,
        out_shape=out_shape,
        scratch_shapes=[pltpu.VMEM((tm, d), BF16)],
        compiler_params=_params(("parallel", "arbitrary"), 48),
        name="qkv",
    )(x, g.reshape(1, d), w_qk, w_v, gains)


def _z_dt_kernel(x_ref, g_ref, w_ref, wdt_ref, b_ref, z_ref, dt_ref, *rest, emit_bf16):
    xn_ref = rest[-1]

    def project():
        if emit_bf16:
            w = rest[0]
            w[...] = w_ref[...].astype(BF16)
        else:
            w = w_ref
        z_ref[...] = jnp.dot(xn_ref[...], w[...], preferred_element_type=F32).astype(z_ref.dtype)

    def project_with_dt():
        raw = jnp.dot(xn_ref[...], wdt_ref[...], preferred_element_type=F32) + b_ref[...]
        dt_ref[...] = jnp.maximum(raw, 0.0) + jnp.log1p(jnp.exp(-jnp.abs(raw)))
        project()

    first = pl.program_id(1) == 0

    @pl.when(first)
    def _():
        x = x_ref[...]
        xn_ref[...] = (x * _rms_scale(x) * g_ref[...]).astype(BF16)
        project_with_dt()

    @pl.when(jnp.logical_not(first))
    def _():
        project()


def _ssd_z_dt(x, g, w_z, layer, w_dt, dt_layer, dt_bias, *, tm, emit_bf16=False):
    m, d = x.shape
    assert not emit_bf16 or m == tm
    out_specs = [
        pl.BlockSpec((tm, COL_TILE), lambda i, j: (i, j)),
        pl.BlockSpec((tm, LANES), lambda i, j: (i, 0)),
    ]
    out_shape = [
        jax.ShapeDtypeStruct((m, D_INNER), BF16),
        jax.ShapeDtypeStruct((m, LANES), F32),
    ]
    if emit_bf16:
        out_specs.append(pl.BlockSpec((d, COL_TILE), lambda i, j: (0, j)))
        out_shape.append(jax.ShapeDtypeStruct((d, D_INNER), BF16))
    return pl.pallas_call(
        functools.partial(_z_dt_kernel, emit_bf16=emit_bf16),
        grid=(m // tm, D_INNER // COL_TILE),
        in_specs=[
            pl.BlockSpec((tm, d), lambda i, j: (i, 0)),
            pl.BlockSpec((1, d), lambda i, j: (0, 0)),
            pl.BlockSpec((None, d, COL_TILE), lambda i, j: (layer, 0, j)),
            pl.BlockSpec((None, d, LANES), lambda i, j: (dt_layer, 0, 0)),
            pl.BlockSpec((1, LANES), lambda i, j: (0, 0)),
        ],
        out_specs=out_specs,
        out_shape=out_shape,
        scratch_shapes=[pltpu.VMEM((tm, d), BF16)],
        compiler_params=_params(("parallel", "arbitrary"), 48),
        name="ssd_z_dt",
    )(x, g.reshape(1, d), w_z, w_dt, dt_bias)


def _xbc_kernel(x_ref, g_ref, w_ref, cw_ref, cb_ref, init_ref, o_ref, st_ref, *rest, tm, sub, row_block, streams,
                emit_bf16):
    xn_ref, carry_ref = rest[-2:]
    i = pl.program_id(1)
    n = pl.program_id(2)
    chained = streams == 1

    if chained:
        @pl.when(i == 0)
        def _():
            carry_ref[n] = init_ref[0]

    rb = row_block

    def work():
        if emit_bf16:
            w = rest[0]
            w[...] = w_ref[...].astype(BF16)
        else:
            w = w_ref
        for c in range(w_ref.shape[1] // sub):
            cs = slice(c * sub, (c + 1) * sub)
            taps = [cw_ref[t:t + 1, cs] for t in range(CONV_WIDTH)]
            carry = carry_ref[n, :, cs] if chained else None
            for r in range(tm // rb):
                rows = slice(r * rb, (r + 1) * rb)
                if not chained:
                    carry = init_ref[r, :, cs]
                raw = jnp.dot(xn_ref[rows, :], w[:, cs], preferred_element_type=F32)
                conv = cb_ref[:, cs] + raw * taps[CONV_WIDTH - 1]
                for t in range(CONV_WIDTH - 1):
                    conv = conv + pltpu.roll(raw, CONV_WIDTH - 1 - t, 0) * taps[t]
                o_ref[rows, cs] = _silu(conv).astype(o_ref.dtype)
                head = jnp.concatenate([carry, raw[0:SUBLANES]], axis=0)
                top = cb_ref[:, cs] + head[SUBLANES:] * taps[CONV_WIDTH - 1]
                for t in range(CONV_WIDTH - 1):
                    lag = CONV_WIDTH - 1 - t
                    top = top + head[SUBLANES - lag:2 * SUBLANES - lag] * taps[t]
                o_ref[r * rb:r * rb + SUBLANES, cs] = _silu(top).astype(o_ref.dtype)
                carry = raw[rb - SUBLANES:rb]
                if not chained:
                    st_ref[r, :, cs] = carry
            if chained:
                carry_ref[n, :, cs] = carry
                st_ref[0, :, cs] = carry

    _with_norm_on_first(x_ref, g_ref, xn_ref, n == 0, work)


def _ssd_xbc(x, g, w_xbc, layer, col0, conv_w, conv_b, init, *, tm, emit_bf16=False):
    b, l, d = x.shape
    streams = init.shape[1]
    n_tiles = CONV_DIM // COL_TILE
    tile0 = col0 // COL_TILE
    assert streams == 1 or l == tm
    assert not emit_bf16 or (b == 1 and l == tm)
    rb = min(tm, 512) if streams == 1 else tm // streams
    out_specs = [
        pl.BlockSpec((None, tm, COL_TILE), lambda bb, i, n: (bb, i, n)),
        pl.BlockSpec((None, None, streams, SUBLANES, COL_TILE), lambda bb, i, n: (bb, i, 0, 0, n)),
    ]
    out_shape = [
        jax.ShapeDtypeStruct((b, l, CONV_DIM), BF16),
        jax.ShapeDtypeStruct((b, l // tm, streams, SUBLANES, CONV_DIM), F32),
    ]
    if emit_bf16:
        out_specs.append(pl.BlockSpec((d, COL_TILE), lambda bb, i, n: (0, n)))
        out_shape.append(jax.ShapeDtypeStruct((d, CONV_DIM), BF16))
    return pl.pallas_call(
        functools.partial(_xbc_kernel, tm=tm, sub=MXU_DIM, row_block=rb, streams=streams, emit_bf16=emit_bf16),
        grid=(b, l // tm, n_tiles),
        in_specs=[
            pl.BlockSpec((None, tm, d), lambda bb, i, n: (bb, i, 0)),
            pl.BlockSpec((1, d), lambda bb, i, n: (0, 0)),
            pl.BlockSpec((None, d, COL_TILE), lambda bb, i, n: (layer, 0, tile0 + n)),
            pl.BlockSpec((CONV_WIDTH, COL_TILE), lambda bb, i, n: (0, n)),
            pl.BlockSpec((1, COL_TILE), lambda bb, i, n: (0, n)),
            pl.BlockSpec((None, streams, SUBLANES, COL_TILE), lambda bb, i, n: (bb, 0, 0, n)),
        ],
        out_specs=out_specs,
        out_shape=out_shape,
        scratch_shapes=[
            pltpu.VMEM((tm, d), BF16),
            pltpu.VMEM((n_tiles, SUBLANES, COL_TILE), F32),
        ],
        compiler_params=_params(("arbitrary", "arbitrary", "arbitrary"), 48),
        name="ssd_xbc",
    )(x, g.reshape(1, d), w_xbc, conv_w, conv_b.reshape(1, CONV_DIM), init)


def _proj_res_kernel(a_ref, w_ref, x_ref, o_ref):
    o_ref[...] = x_ref[...] + jnp.dot(a_ref[...], w_ref[...], preferred_element_type=F32)


def _proj_res(a, w, layer, x, *, tm):
    m, k = a.shape
    n = w.shape[2]
    return pl.pallas_call(
        _proj_res_kernel,
        grid=(n // COL_TILE, m // tm),
        in_specs=[
            pl.BlockSpec((tm, k), lambda j, i: (i, 0)),
            pl.BlockSpec((None, k, COL_TILE), lambda j, i: (layer, 0, j)),
            pl.BlockSpec((tm, COL_TILE), lambda j, i: (i, j)),
        ],
        out_specs=pl.BlockSpec((tm, COL_TILE), lambda j, i: (i, j)),
        out_shape=jax.ShapeDtypeStruct((m, n), F32),
        compiler_params=_params(("parallel", "parallel"), 56),
        name="proj_res",
    )(a, w, x)


def _band_attn_kernel(q_ref, k0_ref, k1_ref, k2_ref, v0_ref, v1_ref, v2_ref, bias_ref, o_ref):
    k_refs = (k0_ref, k1_ref, k2_ref)
    v_refs = (v0_ref, v1_ref, v2_ref)
    for h in range(N_HEADS):
        cols = slice(h * HEAD_DIM, (h + 1) * HEAD_DIM)
        qh = q_ref[:, cols]
        scores = [lax.dot_general(qh, k_refs[t][:, cols], NT_DIMS, preferred_element_type=F32)
                  + bias_ref[h, :, t * Q_BLOCK:(t + 1) * Q_BLOCK] for t in range(KV_BLOCKS)]
        row_max = jnp.max(functools.reduce(jnp.maximum, scores), axis=-1, keepdims=True)
        probs = [jnp.exp2(s - row_max) for s in scores]
        denom = jnp.sum(functools.reduce(jnp.add, probs), axis=-1, keepdims=True)
        out = functools.reduce(jnp.add, [
            jnp.dot(p.astype(BF16), v_refs[t][:, cols], preferred_element_type=F32)
            for t, p in enumerate(probs)])
        o_ref[:, cols] = (out / denom).astype(o_ref.dtype)


def _band_attn(qk, v, bias):
    b, l, _ = v.shape

    def kv_spec(col, t):
        return pl.BlockSpec((None, Q_BLOCK, D_MODEL),
                            lambda bb, j: (bb, jnp.maximum(j - (KV_BLOCKS - 1 - t), 0), col))

    return pl.pallas_call(
        _band_attn_kernel,
        grid=(b, l // Q_BLOCK),
        in_specs=[pl.BlockSpec((None, Q_BLOCK, D_MODEL), lambda bb, j: (bb, j, 0))]
        + [kv_spec(1, t) for t in range(KV_BLOCKS)]
        + [kv_spec(0, t) for t in range(KV_BLOCKS)]
        + [pl.BlockSpec((None,) + bias.shape[1:], lambda bb, j: (jnp.minimum(j, KV_BLOCKS - 1), 0, 0, 0),
                        pipeline_mode=pl.Buffered(1))],
        out_specs=pl.BlockSpec((None, Q_BLOCK, D_MODEL), lambda bb, j: (bb, j, 0)),
        out_shape=jax.ShapeDtypeStruct((b, l, D_MODEL), BF16),
        compiler_params=_params(("parallel", "parallel"), 48),
        name="band_attn",
    )(qk, qk, qk, qk, v, v, v, bias)


def _step_attn_kernel(qk_ref, v_ref, kc_ref, vc_ref, bias_c_ref, bias_n_ref, o_ref):
    for h in range(N_HEADS):
        cols = slice(h * HEAD_DIM, (h + 1) * HEAD_DIM)
        qh = qk_ref[:, cols].astype(BF16)
        kn = qk_ref[:, D_MODEL + h * HEAD_DIM:D_MODEL + (h + 1) * HEAD_DIM].astype(BF16)
        vn = v_ref[:, cols].astype(BF16)
        kc = kc_ref[:, h, :].astype(BF16)
        vc = vc_ref[:, h, :].astype(BF16)
        s_c = lax.dot_general(qh, kc, NT_DIMS, preferred_element_type=F32) + bias_c_ref[h]
        s_n = lax.dot_general(qh, kn, NT_DIMS, preferred_element_type=F32) + bias_n_ref[h]
        row_max = jnp.maximum(jnp.max(s_c, axis=-1, keepdims=True), jnp.max(s_n, axis=-1, keepdims=True))
        p_c = jnp.exp2(s_c - row_max)
        p_n = jnp.exp2(s_n - row_max)
        denom = jnp.sum(p_c, axis=-1, keepdims=True) + jnp.sum(p_n, axis=-1, keepdims=True)
        out = (jnp.dot(p_c.astype(BF16), vc, preferred_element_type=F32)
               + jnp.dot(p_n.astype(BF16), vn, preferred_element_type=F32))
        o_ref[:, cols] = (out / denom).astype(o_ref.dtype)


def _step_attn(qk, v, k_cache, v_cache, layer, bias_c, bias_n):
    b, s, _ = v.shape
    w = k_cache.shape[2]
    cache_spec = pl.BlockSpec((None, None, w, N_HEADS, HEAD_DIM), lambda i: (layer, i, 0, 0, 0))
    return pl.pallas_call(
        _step_attn_kernel,
        grid=(b,),
        in_specs=[
            pl.BlockSpec((None, s, 2 * D_MODEL), lambda i: (i, 0, 0)),
            pl.BlockSpec((None, s, D_MODEL), lambda i: (i, 0, 0)),
            cache_spec,
            cache_spec,
            pl.BlockSpec(bias_c.shape, lambda i: (0, 0, 0)),
            pl.BlockSpec(bias_n.shape, lambda i: (0, 0, 0)),
        ],
        out_specs=pl.BlockSpec((None, s, D_MODEL), lambda i: (i, 0, 0)),
        out_shape=jax.ShapeDtypeStruct((b, s, D_MODEL), BF16),
        compiler_params=_params(("parallel",), 48),
        name="step_attn",
    )(qk, v, k_cache, v_cache, bias_c, bias_n)


def _lag_vector(table, rows, width, dist0):
    period = width + rows
    m = np.arange(period)
    lag = np.where(m < width, m, m - period)
    idx = np.clip(dist0 - lag, -REL_CLIP, REL_CLIP) + REL_CLIP
    return table[:, idx] * LOG2E


def _skewed_bias(table, rows, width, dist0):
    u = _lag_vector(table, rows, width, dist0)
    period = u.shape[1]
    return jnp.tile(u, (1, rows))[:, :rows * (period - 1)].reshape(-1, rows, period - 1)[:, :, :width]


def _band_bias_kernel(u_ref, o_ref):
    n_variants, rows, width = o_ref.shape
    skew = pltpu.roll(jnp.broadcast_to(u_ref[...], (rows, u_ref.shape[-1])), 0, 1, stride=1, stride_axis=0)
    skew = skew[:, :width]
    qi = lax.broadcasted_iota(jnp.int32, (rows, width), 0)
    kn = lax.broadcasted_iota(jnp.int32, (rows, width), 1)
    chunk_gap = qi // CHUNK + (n_variants - 1) * (rows // CHUNK) - kn // CHUNK
    in_band = (chunk_gap >= 0) & (chunk_gap <= LEFT_CHUNKS)
    for v in range(n_variants):
        visible = in_band & (kn // rows >= n_variants - 1 - v)
        o_ref[v] = jnp.where(visible, skew, NEG_INF)


def _band_bias(table):
    width = KV_BLOCKS * Q_BLOCK
    u = _lag_vector(table, Q_BLOCK, width, (KV_BLOCKS - 1) * Q_BLOCK).astype(F32)
    n_heads, period = u.shape
    return pl.pallas_call(
        _band_bias_kernel,
        grid=(n_heads,),
        in_specs=[pl.BlockSpec((None, 1, period), lambda h: (h, 0, 0))],
        out_specs=pl.BlockSpec((KV_BLOCKS, None, Q_BLOCK, width), lambda h: (0, h, 0, 0)),
        out_shape=jax.ShapeDtypeStruct((KV_BLOCKS, n_heads, Q_BLOCK, width), F32),
        compiler_params=_params(("parallel",), 32),
        name="band_bias",
    )(u.reshape(n_heads, 1, period))


def _step_bias(table, s, w):
    q_pos = PAST_LEN + np.arange(s)
    k_pos = np.concatenate([PAST_LEN - w + np.arange(w), q_pos])
    qch, kch = q_pos // CHUNK, k_pos // CHUNK
    mask = (kch[None, :] <= qch[:, None]) & (kch[None, :] >= qch[:, None] - LEFT_CHUNKS)
    bias_c = jnp.where(mask[None, :, :w], _skewed_bias(table, s, w, w), NEG_INF).astype(F32)
    bias_n = jnp.where(mask[None, :, w:], _skewed_bias(table, s, s, 0), NEG_INF).astype(F32)
    return bias_c, bias_n


def _ssd_scan_kernel(xs_ref, b_ref, c_ref, dt_ref, z_ref, h0_ref, a_ref, dskip_ref, gg_ref,
                     yn_ref, hout_ref, state_ref, y_ref, src_t_ref, w_t_ref, *, rows):
    lc = SCAN_CHUNK
    i = pl.program_id(1)

    @pl.when(i == 0)
    def _():
        state_ref[...] = h0_ref[...]

    row_id = lax.broadcasted_iota(jnp.int32, (lc, lc), 0)
    col_id = lax.broadcasted_iota(jnp.int32, (lc, lc), 1)
    causal = col_id <= row_id
    causal_f = causal.astype(F32)
    low_half = lax.broadcasted_iota(jnp.int32, (lc, PAIR), 1) < SSM_HEADDIM
    low_half_row = lax.broadcasted_iota(jnp.int32, (1, PAIR), 1) < SSM_HEADDIM

    def chunk(c, carry):
        rows_c = pl.ds(pl.multiple_of(c * lc, lc), lc)
        dt = dt_ref[rows_c, :]
        a_cs = jnp.dot(causal_f, dt * (a_ref[...] * LOG2E), precision=lax.Precision.HIGHEST,
                       preferred_element_type=F32)
        a_cs_t = a_cs.T
        dt_t = dt.T
        src_t_ref[...] = a_cs_t - jnp.log2(dt_t)
        w_t_ref[...] = dt_t * jnp.exp2(a_cs_t[:, lc - 1:lc] - a_cs_t)
        chunk_decay = jnp.exp2(a_cs[lc - 1:lc, :])
        for g in range(N_GROUPS):
            gcols = slice(g * D_STATE, (g + 1) * D_STATE)
            bg = b_ref[rows_c, gcols]
            cg = c_ref[rows_c, gcols]
            cb = lax.dot_general(cg, bg, NT_DIMS, preferred_element_type=F32)
            bg_t = bg.astype(F32).T
            cg_f = cg.astype(F32)
            for jp in range(PAIRS_PER_GROUP):
                q = g * PAIRS_PER_GROUP + jp
                pcols = slice(q * PAIR, (q + 1) * PAIR)
                xp = xs_ref[rows_c, pcols]
                h_t = state_ref[q]
                lhs_y, lhs_s = [], []
                for h in (2 * q, 2 * q + 1):
                    col = jnp.broadcast_to(a_cs[:, h:h + 1], (lc, lc))
                    row = jnp.broadcast_to(src_t_ref[h:h + 1, :], (lc, lc))
                    within = cb * jnp.exp2(jnp.where(causal, col - row, -jnp.inf))
                    carried = cg_f * jnp.exp2(col)
                    lhs_y.append(jnp.concatenate([within.astype(BF16), carried.astype(BF16)], axis=1))
                    lhs_s.append((bg_t * jnp.broadcast_to(w_t_ref[h:h + 1, :], (D_STATE, lc))).astype(BF16))
                y2 = jnp.dot(jnp.concatenate(lhs_y, axis=0), jnp.concatenate([xp, h_t.astype(BF16)], axis=0),
                             preferred_element_type=F32)
                s2 = jnp.dot(jnp.concatenate(lhs_s, axis=0), xp, preferred_element_type=F32)
                decay = jnp.where(low_half_row,
                                  jnp.broadcast_to(chunk_decay[:, 2 * q:2 * q + 1], (1, PAIR)),
                                  jnp.broadcast_to(chunk_decay[:, 2 * q + 1:2 * q + 2], (1, PAIR)))
                state_ref[q] = h_t * decay + jnp.where(low_half, s2[:D_STATE], s2[D_STATE:])
                y_ref[rows_c, pcols] = (jnp.where(low_half, y2[:lc], y2[lc:])
                                        + dskip_ref[:, pcols] * xp.astype(F32))
        return carry

    lax.fori_loop(0, rows // lc, chunk, 0)

    y = y_ref[...] * _silu(z_ref[...].astype(F32))
    yn_ref[...] = (y * _rms_scale(y) * gg_ref[...]).astype(yn_ref.dtype)

    @pl.when(i == pl.num_programs(1) - 1)
    def _():
        hout_ref[...] = state_ref[...]


def _ssd_scan(xbc, dt, z, h0, a_neg, d_skip, gate_g, *, rows):
    b, l, _ = xbc.shape
    gn = N_GROUPS * D_STATE
    b_block = D_INNER // gn
    return pl.pallas_call(
        functools.partial(_ssd_scan_kernel, rows=rows),
        grid=(b, l // rows),
        in_specs=[
            pl.BlockSpec((None, rows, D_INNER), lambda bb, i: (bb, i, 0)),
            pl.BlockSpec((None, rows, gn), lambda bb, i: (bb, i, b_block)),
            pl.BlockSpec((None, rows, gn), lambda bb, i: (bb, i, b_block + 1)),
            pl.BlockSpec((None, rows, LANES), lambda bb, i: (bb, i, 0)),
            pl.BlockSpec((None, rows, D_INNER), lambda bb, i: (bb, i, 0)),
            pl.BlockSpec((None, N_PAIRS, D_STATE, PAIR), lambda bb, i: (bb, 0, 0, 0)),
            pl.BlockSpec((1, LANES), lambda bb, i: (0, 0)),
            pl.BlockSpec((1, D_INNER), lambda bb, i: (0, 0)),
            pl.BlockSpec((1, D_INNER), lambda bb, i: (0, 0)),
        ],
        out_specs=[
            pl.BlockSpec((None, rows, D_INNER), lambda bb, i: (bb, i, 0)),
            pl.BlockSpec((None, N_PAIRS, D_STATE, PAIR), lambda bb, i: (bb, 0, 0, 0)),
        ],
        out_shape=[
            jax.ShapeDtypeStruct((b, l, D_INNER), BF16),
            jax.ShapeDtypeStruct((b, N_PAIRS, D_STATE, PAIR), F32),
        ],
        scratch_shapes=[
            pltpu.VMEM((N_PAIRS, D_STATE, PAIR), F32),
            pltpu.VMEM((rows, D_INNER), F32),
            pltpu.VMEM((LANES, SCAN_CHUNK), F32),
            pltpu.VMEM((LANES, SCAN_CHUNK), F32),
        ],
        compiler_params=_params(("arbitrary", "arbitrary"), 48),
        name="ssd_scan",
    )(xbc, xbc, xbc, dt, z, h0, a_neg, d_skip, gate_g)


def _to_pair_layout(h):
    b = h.shape[0]
    return h.reshape(b, N_PAIRS, 2, SSM_HEADDIM, D_STATE).transpose(0, 1, 4, 2, 3).reshape(b, N_PAIRS, D_STATE, PAIR)


def _from_pair_layout(h):
    b = h.shape[0]
    return (h.reshape(b, N_PAIRS, D_STATE, 2, SSM_HEADDIM).transpose(0, 1, 3, 4, 2)
            .reshape(b, N_SSM_HEADS, SSM_HEADDIM, D_STATE))


def _ffn_layer(xp, xs, g, w_gu, w_d, layer, half):
    xs, w_g, w_u, w_dn = _ffn_cast(xs, g, w_gu, w_d, layer, half, tf=256)
    xp = _ffn(xp, g, w_g, w_u, w_dn, tm=1024, tf=512, sub=MXU_DIM)
    return xp, xs


def _attn_layer(xp, xs, cache_k, cache_v, g, w_qkv, w_o, layer, q_gain, k_gain, table, *, batch, seq):
    dec_batch, w = cache_k.shape[1], cache_k.shape[2]
    dec_seq = xs.shape[0] // dec_batch
    gains = jnp.stack([q_gain * (HEAD_DIM ** -0.5 * LOG2E), k_gain]).astype(F32)
    m_s = dec_batch * dec_seq
    qk_s, v_s, w_qk, w_v = _qkv(xs, g, w_qkv, w_qkv, layer, 2 * D_MODEL, gains, tm=m_s, out_dtype=F32,
                                emit_bf16=True)
    w_qk, w_v = w_qk[None], w_v[None]
    ks_new = qk_s[:, D_MODEL:].reshape(dec_batch, dec_seq, N_HEADS, HEAD_DIM)
    vs_new = v_s.reshape(dec_batch, dec_seq, N_HEADS, HEAD_DIM)
    bias_c, bias_n = _step_bias(table, dec_seq, w)
    o_s = _step_attn(qk_s.reshape(dec_batch, dec_seq, 2 * D_MODEL), v_s.reshape(dec_batch, dec_seq, D_MODEL),
                     cache_k, cache_v, layer, bias_c, bias_n)
    xs = _proj_res(o_s.reshape(m_s, D_MODEL), w_o, layer, xs, tm=m_s)
    qk, v = _qkv(xp, g, w_qk, w_v, 0, 0, gains, tm=1024, out_dtype=BF16)
    win = min(LEFT_CHUNKS * CHUNK, seq)
    x_tail = xp.reshape(batch, seq, D_MODEL)[:, seq - win:].reshape(batch * win, D_MODEL)
    qk_tail, v_tail = _qkv(x_tail, g, w_qk, w_v, 0, 0, gains, tm=batch * win, out_dtype=F32)
    k_new = qk_tail[:, D_MODEL:].reshape(batch, win, N_HEADS, HEAD_DIM)
    v_new = v_tail.reshape(batch, win, N_HEADS, HEAD_DIM)
    o = _band_attn(qk.reshape(batch, seq, 2 * D_MODEL), v.reshape(batch, seq, D_MODEL), _band_bias(table))
    xp = _proj_res(o.reshape(batch * seq, D_MODEL), w_o, layer, xp, tm=1024)
    return xp, xs, k_new, v_new, ks_new, vs_new


def _ssd_stream(x2d, h0_pairs, conv_init, p, w_z, w_xbc, w_layer, xbc_col0, *, batch, seq, conv_batch, tm, rows,
                emit_bf16=False):
    g, w_out, layer, w_dt, conv_w, conv_b, dt_bias, a_neg, d_skip, gate_g = p
    row_tile = min(1024, batch * seq)
    z, dt, *w_z_bf = _ssd_z_dt(x2d, g, w_z, w_layer, w_dt, layer, dt_bias, tm=row_tile, emit_bf16=emit_bf16)
    xbc, tails, *w_xbc_bf = _ssd_xbc(x2d.reshape(conv_batch, -1, D_MODEL), g, w_xbc, w_layer, xbc_col0, conv_w,
                                     conv_b, conv_init, tm=tm, emit_bf16=emit_bf16)
    xbc = xbc.reshape(batch, seq, CONV_DIM)
    z = z.reshape(batch, seq, D_INNER)
    dt = dt.reshape(batch, seq, LANES)
    pad = (-seq) % SCAN_CHUNK
    if pad:
        widen = lambda t: jnp.pad(t, ((0, 0), (0, pad), (0, 0)))
        xbc, z, dt = widen(xbc), widen(z), widen(dt)
    yn, h_new = _ssd_scan(xbc, dt, z, h0_pairs, a_neg, d_skip, gate_g, rows=rows)
    yn = yn[:, :seq].reshape(batch * seq, D_INNER)
    x2d = _proj_res(yn, w_out, layer, x2d, tm=row_tile)
    conv_state = tails[:, -1, :, SUBLANES - (CONV_WIDTH - 1):].reshape(batch, CONV_WIDTH - 1, CONV_DIM)
    return x2d, _from_pair_layout(h_new), conv_state, w_z_bf + w_xbc_bf


def _ssd_layer(xp, xs, state_ssm, state_conv, g, w_in, w_out, layer, w_dt, conv_w, conv_b, dt_bias, a_log,
               d_skip, gate_g, *, batch, seq):
    dec_batch = state_ssm.shape[0]
    dec_seq = xs.shape[0] // dec_batch
    lane_pad = (0, LANES - N_SSM_HEADS)
    p = (g, w_out, layer, w_dt, conv_w, conv_b,
         jnp.pad(dt_bias.astype(F32), lane_pad).reshape(1, LANES),
         jnp.pad(-jnp.exp(a_log.astype(F32)), lane_pad).reshape(1, LANES),
         jnp.repeat(d_skip, SSM_HEADDIM).reshape(1, D_INNER),
         gate_g.reshape(1, D_INNER))
    conv_init = jnp.pad(state_conv, ((0, 0), (SUBLANES - (CONV_WIDTH - 1), 0), (0, 0)))[None]
    xs, hs, cs, (w_z, w_xbc) = _ssd_stream(
        xs, _to_pair_layout(state_ssm), conv_init, p, w_in, w_in, layer, D_INNER, batch=dec_batch, seq=dec_seq,
        conv_batch=1, tm=dec_batch * dec_seq, rows=SCAN_CHUNK, emit_bf16=True)
    zero_state = jnp.zeros((batch, N_PAIRS, D_STATE, PAIR), F32)
    zero_conv = jnp.zeros((batch, 1, SUBLANES, CONV_DIM), F32)
    xp, hp, cp, _ = _ssd_stream(xp, zero_state, zero_conv, p, w_z[None], w_xbc[None], 0, 0, batch=batch, seq=seq,
                                conv_batch=batch, tm=1024, rows=2 * SCAN_CHUNK)
    return xp, xs, hp, cp, hs, cs


def kernel(x_prompt, x_sample, cache_k, cache_v, state_ssm, state_conv, ffn_norm, ffn_w_gate_up, ffn_w_down, attn_norm, attn_w_qkv, attn_q_norm, attn_k_norm, attn_rel_bias, attn_w_o, ssd_norm, ssd_w_in, ssd_conv_w, ssd_conv_b, ssd_dt_bias, ssd_a_log, ssd_d_skip, ssd_gate_norm, ssd_w_out):
    batch, seq, _ = x_prompt.shape
    dec_batch, dec_seq, _ = x_sample.shape
    depth = ffn_norm.shape[0]
    xp = x_prompt.reshape(batch * seq, D_MODEL)
    xs = x_sample.reshape(dec_batch * dec_seq, D_MODEL)

    w_o = attn_w_o.astype(BF16)
    w_out = ssd_w_out.astype(BF16)
    w_dt = jnp.pad(ssd_w_in[:, :, D_INNER + CONV_DIM:], ((0, 0), (0, 0), (0, LANES - N_SSM_HEADS))).astype(BF16)

    kp, vp, hp, cp, ks, vs, hs, cs = [], [], [], [], [], [], [], []
    for i in range(depth):
        j = i // 2
        xp, xs = _ffn_layer(xp, xs, ffn_norm[i, 0], ffn_w_gate_up, ffn_w_down, i, 0)
        if i % 2 == 0:
            xp, xs, k_new, v_new, ks_new, vs_new = _attn_layer(
                xp, xs, cache_k, cache_v, attn_norm[j], attn_w_qkv, w_o, j, attn_q_norm[j], attn_k_norm[j],
                attn_rel_bias[j], batch=batch, seq=seq)
            kp.append(k_new); vp.append(v_new); ks.append(ks_new); vs.append(vs_new)
        else:
            xp, xs, h_p, c_p, h_s, c_s = _ssd_layer(
                xp, xs, state_ssm[j], state_conv[j], ssd_norm[j], ssd_w_in, w_out, j,
                w_dt, ssd_conv_w[j], ssd_conv_b[j], ssd_dt_bias[j], ssd_a_log[j],
                ssd_d_skip[j], ssd_gate_norm[j], batch=batch, seq=seq)
            hp.append(h_p); cp.append(c_p); hs.append(h_s); cs.append(c_s)
        xp, xs = _ffn_layer(xp, xs, ffn_norm[i, 1], ffn_w_gate_up, ffn_w_down, i, 1)

    return (xp.reshape(batch, seq, D_MODEL), xs.reshape(dec_batch, dec_seq, D_MODEL),
            jnp.stack(kp), jnp.stack(vp), jnp.stack(hp), jnp.stack(cp),
            jnp.stack(ks), jnp.stack(vs), jnp.stack(hs), jnp.stack(cs))
```

```python
import functools
import math

import jax
import jax.numpy as jnp
import numpy as np
from jax import lax
from jax.experimental import pallas as pl
from jax.experimental.pallas import tpu as pltpu

F32 = jnp.float32
BF16 = jnp.bfloat16

D_MODEL = 2048
CHUNK = 64
LEFT_CHUNKS = 8
HEAD_DIM = 128
N_HEADS = D_MODEL // HEAD_DIM
REL_CLIP = 128
PAST_LEN = 1024
D_INNER = 2 * D_MODEL
SSM_HEADDIM = 64
N_SSM_HEADS = D_INNER // SSM_HEADDIM
N_GROUPS = 8
D_STATE = 128
CONV_WIDTH = 4
CONV_DIM = D_INNER + 2 * N_GROUPS * D_STATE
EPS = 1e-6
NEG_INF = -1e30
LOG2E = math.log2(math.e)

LANES = 128
SUBLANES = 8
MXU_DIM = 256
MIB = 1024 * 1024

Q_BLOCK = 4 * CHUNK
KV_BLOCKS = LEFT_CHUNKS * CHUNK // Q_BLOCK + 1
SCAN_CHUNK = 128
PAIR = 2 * SSM_HEADDIM
N_PAIRS = N_SSM_HEADS // 2
PAIRS_PER_GROUP = N_PAIRS // N_GROUPS
COL_TILE = 1024
assert SCAN_CHUNK == D_STATE == PAIR == LANES

NT_DIMS = (((1,), (1,)), ((), ()))


def _params(semantics, vmem_mib):
    return pltpu.CompilerParams(dimension_semantics=semantics, vmem_limit_bytes=vmem_mib * MIB)


def _rms_scale(x):
    return lax.rsqrt(jnp.mean(x * x, axis=-1, keepdims=True) + EPS)


def _silu(x):
    return x * jax.nn.sigmoid(x)


def _with_norm_on_first(x_ref, g_ref, xn_ref, first, work):
    @pl.when(first)
    def _():
        x = x_ref[...]
        xn_ref[...] = (x * _rms_scale(x) * g_ref[...]).astype(BF16)
        work()

    @pl.when(jnp.logical_not(first))
    def _():
        work()


def _ffn_cast_kernel(x_ref, g_ref, wg_ref, wu_ref, wd_ref, o_ref, wgb_ref, wub_ref, wdb_ref, xn_ref):
    @pl.when(pl.program_id(0) == 0)
    def _():
        x = x_ref[...]
        xn_ref[...] = (x * _rms_scale(x) * g_ref[...]).astype(BF16)
        o_ref[...] = x

    wg = wg_ref[...].astype(BF16)
    wu = wu_ref[...].astype(BF16)
    wd = wd_ref[...].astype(BF16)
    wgb_ref[...] = wg
    wub_ref[...] = wu
    wdb_ref[...] = wd
    xn = xn_ref[...]
    gate = jnp.dot(xn, wg, preferred_element_type=F32)
    up = jnp.dot(xn, wu, preferred_element_type=F32)
    act = (_silu(gate) * up).astype(BF16)
    o_ref[...] += 0.5 * jnp.dot(act, wd, preferred_element_type=F32)


def _ffn_cast(x, g, w_gu, w_d, layer, half, *, tf):
    m, d = x.shape
    d_ff = w_d.shape[2]
    nf = d_ff // tf
    return pl.pallas_call(
        _ffn_cast_kernel,
        grid=(nf,),
        in_specs=[
            pl.BlockSpec((m, d), lambda f: (0, 0)),
            pl.BlockSpec((1, d), lambda f: (0, 0)),
            pl.BlockSpec((None, None, d, tf), lambda f: (layer, half, 0, f)),
            pl.BlockSpec((None, None, d, tf), lambda f: (layer, half, 0, f + nf)),
            pl.BlockSpec((None, None, tf, d), lambda f: (layer, half, f, 0)),
        ],
        out_specs=[
            pl.BlockSpec((m, d), lambda f: (0, 0)),
            pl.BlockSpec((d, tf), lambda f: (0, f)),
            pl.BlockSpec((d, tf), lambda f: (0, f)),
            pl.BlockSpec((tf, d), lambda f: (f, 0)),
        ],
        out_shape=[
            jax.ShapeDtypeStruct((m, d), F32),
            jax.ShapeDtypeStruct((d, d_ff), BF16),
            jax.ShapeDtypeStruct((d, d_ff), BF16),
            jax.ShapeDtypeStruct((d_ff, d), BF16),
        ],
        scratch_shapes=[pltpu.VMEM((m, d), BF16)],
        compiler_params=_params(("arbitrary",), 56),
        name="ffn_cast",
    )(x, g.reshape(1, d), w_gu, w_gu, w_d)


def _ffn_kernel(x_ref, g_ref, wg_ref, wu_ref, wd_ref, o_ref, xn_ref, *, sub):
    def half_down():
        xn = xn_ref[...]
        down = None
        for c in range(wg_ref.shape[1] // sub):
            cs = slice(c * sub, (c + 1) * sub)
            gate = jnp.dot(xn, wg_ref[:, cs], preferred_element_type=F32)
            up = jnp.dot(xn, wu_ref[:, cs], preferred_element_type=F32)
            act = (_silu(gate) * up).astype(BF16)
            part = jnp.dot(act, wd_ref[cs, :], preferred_element_type=F32)
            down = part if down is None else down + part
        return 0.5 * down

    first = pl.program_id(1) == 0

    @pl.when(first)
    def _():
        x = x_ref[...]
        xn_ref[...] = (x * _rms_scale(x) * g_ref[...]).astype(BF16)
        o_ref[...] = x_ref[...] + half_down()

    @pl.when(jnp.logical_not(first))
    def _():
        o_ref[...] += half_down()


def _ffn(x, g, w_g, w_u, w_d, *, tm, tf, sub):
    m, d = x.shape
    d_ff = w_d.shape[0]
    return pl.pallas_call(
        functools.partial(_ffn_kernel, sub=sub),
        grid=(m // tm, d_ff // tf),
        in_specs=[
            pl.BlockSpec((tm, d), lambda i, f: (i, 0)),
            pl.BlockSpec((1, d), lambda i, f: (0, 0)),
            pl.BlockSpec((d, tf), lambda i, f: (0, f)),
            pl.BlockSpec((d, tf), lambda i, f: (0, f)),
            pl.BlockSpec((tf, d), lambda i, f: (f, 0)),
        ],
        out_specs=pl.BlockSpec((tm, d), lambda i, f: (i, 0)),
        out_shape=jax.ShapeDtypeStruct((m, d), F32),
        scratch_shapes=[pltpu.VMEM((tm, d), BF16)],
        compiler_params=_params(("parallel", "arbitrary"), 58),
        name="ffn",
    )(x, g.reshape(1, d), w_g, w_u, w_d)


def _qkv_kernel(x_ref, g_ref, wqk_ref, wv_ref, gain_ref, qk_ref, v_ref, *rest, sub, emit_bf16):
    xn_ref = rest[-1]
    n = pl.program_id(1)

    def work():
        if emit_bf16:
            wqk, wv = rest[0], rest[1]
            wqk[...] = wqk_ref[...].astype(BF16)
            wv[...] = wv_ref[...].astype(BF16)
        else:
            wqk, wv = wqk_ref, wv_ref
        xn = xn_ref[...]
        gain = gain_ref[pl.ds(n // (pl.num_programs(1) // 2), 1), :]
        for c in range(wqk_ref.shape[1] // sub):
            acc = jnp.dot(xn, wqk[:, c * sub:(c + 1) * sub], preferred_element_type=F32)
            for h in range(sub // HEAD_DIM):
                a = acc[:, h * HEAD_DIM:(h + 1) * HEAD_DIM]
                cols = slice(c * sub + h * HEAD_DIM, c * sub + (h + 1) * HEAD_DIM)
                qk_ref[:, cols] = (a * _rms_scale(a) * gain).astype(qk_ref.dtype)
        v_ref[...] = jnp.dot(xn, wv[...], preferred_element_type=F32).astype(v_ref.dtype)

    _with_norm_on_first(x_ref, g_ref, xn_ref, n == 0, work)


def _qkv(x, g, w_qk, w_v, layer, v_col0, gains, *, tm, out_dtype, emit_bf16=False):
    m, d = x.shape
    n_tiles = 2 * D_MODEL // COL_TILE
    tv = D_MODEL // n_tiles
    v0 = v_col0 // tv
    assert not emit_bf16 or m == tm
    out_specs = [
        pl.BlockSpec((tm, COL_TILE), lambda i, n: (i, n)),
        pl.BlockSpec((tm, tv), lambda i, n: (i, n)),
    ]
    out_shape = [
        jax.ShapeDtypeStruct((m, 2 * D_MODEL), out_dtype),
        jax.ShapeDtypeStruct((m, D_MODEL), out_dtype),
    ]
    if emit_bf16:
        out_specs += [pl.BlockSpec((d, COL_TILE), lambda i, n: (0, n)), pl.BlockSpec((d, tv), lambda i, n: (0, n))]
        out_shape += [jax.ShapeDtypeStruct((d, 2 * D_MODEL), BF16), jax.ShapeDtypeStruct((d, D_MODEL), BF16)]
    return pl.pallas_call(
        functools.partial(_qkv_kernel, sub=MXU_DIM, emit_bf16=emit_bf16),
        grid=(m // tm, n_tiles),
        in_specs=[
            pl.BlockSpec((tm, d), lambda i, n: (i, 0)),
            pl.BlockSpec((1, d), lambda i, n: (0, 0)),
            pl.BlockSpec((None, d, COL_TILE), lambda i, n: (layer, 0, n)),
            pl.BlockSpec((None, d, tv), lambda i, n: (layer, 0, v0 + n)),
            pl.BlockSpec((2, HEAD_DIM), lambda i, n: (0, 0)),
        ],
        out_specs=out_specs,
        out_shape=out_shape,
        scratch_shapes=[pltpu.VMEM((tm, d), BF16)],
        compiler_params=_params(("parallel", "arbitrary"), 48),
        name="qkv",
    )(x, g.reshape(1, d), w_qk, w_v, gains)


def _z_dt_kernel(x_ref, g_ref, w_ref, wdt_ref, b_ref, z_ref, dt_ref, xn_ref):
    def project():
        z_ref[...] = jnp.dot(xn_ref[...], w_ref[...], preferred_element_type=F32).astype(z_ref.dtype)

    def project_with_dt():
        raw = jnp.dot(xn_ref[...], wdt_ref[...], preferred_element_type=F32) + b_ref[...]
        dt_ref[...] = jnp.maximum(raw, 0.0) + jnp.log1p(jnp.exp(-jnp.abs(raw)))
        project()

    first = pl.program_id(1) == 0

    @pl.when(first)
    def _():
        x = x_ref[...]
        xn_ref[...] = (x * _rms_scale(x) * g_ref[...]).astype(BF16)
        project_with_dt()

    @pl.when(jnp.logical_not(first))
    def _():
        project()


def _ssd_z_dt(x, g, w_in, layer, w_dt, dt_bias, *, tm):
    m, d = x.shape
    return pl.pallas_call(
        _z_dt_kernel,
        grid=(m // tm, D_INNER // COL_TILE),
        in_specs=[
            pl.BlockSpec((tm, d), lambda i, j: (i, 0)),
            pl.BlockSpec((1, d), lambda i, j: (0, 0)),
            pl.BlockSpec((None, d, COL_TILE), lambda i, j: (layer, 0, j)),
            pl.BlockSpec((None, d, LANES), lambda i, j: (layer, 0, 0)),
            pl.BlockSpec((1, LANES), lambda i, j: (0, 0)),
        ],
        out_specs=[
            pl.BlockSpec((tm, COL_TILE), lambda i, j: (i, j)),
            pl.BlockSpec((tm, LANES), lambda i, j: (i, 0)),
        ],
        out_shape=[
            jax.ShapeDtypeStruct((m, D_INNER), BF16),
            jax.ShapeDtypeStruct((m, LANES), F32),
        ],
        scratch_shapes=[pltpu.VMEM((tm, d), BF16)],
        compiler_params=_params(("parallel", "arbitrary"), 48),
        name="ssd_z_dt",
    )(x, g.reshape(1, d), w_in, w_dt, dt_bias)


def _xbc_kernel(x_ref, g_ref, w_ref, cw_ref, cb_ref, init_ref, o_ref, st_ref, xn_ref, carry_ref, *, tm, sub,
                row_block, streams):
    i = pl.program_id(1)
    n = pl.program_id(2)
    chained = streams == 1

    if chained:
        @pl.when(i == 0)
        def _():
            carry_ref[n] = init_ref[0]

    rb = row_block
    sublane_id = lax.broadcasted_iota(jnp.int32, (1, SUBLANES, sub), 1)

    def work():
        for c in range(w_ref.shape[1] // sub):
            cs = slice(c * sub, (c + 1) * sub)
            taps = [cw_ref[t:t + 1, cs] for t in range(CONV_WIDTH)]
            carry = carry_ref[n, :, cs] if chained else None
            for r in range(tm // rb):
                rows = slice(r * rb, (r + 1) * rb)
                if not chained:
                    carry = init_ref[r, :, cs]
                raw = jnp.dot(xn_ref[rows, :], w_ref[:, cs], preferred_element_type=F32)
                tiles = jnp.concatenate([carry, raw], axis=0).reshape(rb // SUBLANES + 1, SUBLANES, sub)
                conv = cb_ref[:, cs] + raw * taps[CONV_WIDTH - 1]
                conv = conv.reshape(rb // SUBLANES, SUBLANES, sub)
                for t in range(CONV_WIDTH - 1):
                    lag = CONV_WIDTH - 1 - t
                    rot = pltpu.roll(tiles, lag, 1)
                    conv = conv + jnp.where(sublane_id < lag, rot[:-1], rot[1:]) * taps[t]
                o_ref[rows, cs] = _silu(conv).reshape(rb, sub).astype(o_ref.dtype)
                carry = raw[rb - SUBLANES:rb]
                if not chained:
                    st_ref[r, :, cs] = carry
            if chained:
                carry_ref[n, :, cs] = carry
                st_ref[0, :, cs] = carry

    _with_norm_on_first(x_ref, g_ref, xn_ref, n == 0, work)


def _ssd_xbc(x, g, w_in, layer, conv_w, conv_b, init, *, tm):
    b, l, d = x.shape
    streams = init.shape[1]
    n_tiles = CONV_DIM // COL_TILE
    tile0 = D_INNER // COL_TILE
    assert streams == 1 or l == tm
    rb = min(tm, 512) if streams == 1 else tm // streams
    out_specs = [
        pl.BlockSpec((None, tm, COL_TILE), lambda bb, i, n: (bb, i, n)),
        pl.BlockSpec((None, None, streams, SUBLANES, COL_TILE), lambda bb, i, n: (bb, i, 0, 0, n)),
    ]
    out_shape = [
        jax.ShapeDtypeStruct((b, l, CONV_DIM), BF16),
        jax.ShapeDtypeStruct((b, l // tm, streams, SUBLANES, CONV_DIM), F32),
    ]
    return pl.pallas_call(
        functools.partial(_xbc_kernel, tm=tm, sub=MXU_DIM, row_block=rb, streams=streams),
        grid=(b, l // tm, n_tiles),
        in_specs=[
            pl.BlockSpec((None, tm, d), lambda bb, i, n: (bb, i, 0)),
            pl.BlockSpec((1, d), lambda bb, i, n: (0, 0)),
            pl.BlockSpec((None, d, COL_TILE), lambda bb, i, n: (layer, 0, tile0 + n)),
            pl.BlockSpec((CONV_WIDTH, COL_TILE), lambda bb, i, n: (0, n)),
            pl.BlockSpec((1, COL_TILE), lambda bb, i, n: (0, n)),
            pl.BlockSpec((None, streams, SUBLANES, COL_TILE), lambda bb, i, n: (bb, 0, 0, n)),
        ],
        out_specs=out_specs,
        out_shape=out_shape,
        scratch_shapes=[
            pltpu.VMEM((tm, d), BF16),
            pltpu.VMEM((n_tiles, SUBLANES, COL_TILE), F32),
        ],
        compiler_params=_params(("arbitrary", "arbitrary", "arbitrary"), 48),
        name="ssd_xbc",
    )(x, g.reshape(1, d), w_in, conv_w, conv_b.reshape(1, CONV_DIM), init)


def _proj_res_kernel(a_ref, w_ref, x_ref, o_ref):
    o_ref[...] = x_ref[...] + jnp.dot(a_ref[...], w_ref[...], preferred_element_type=F32)


def _proj_res(a, w, layer, x, *, tm):
    m, k = a.shape
    n = w.shape[2]
    return pl.pallas_call(
        _proj_res_kernel,
        grid=(n // COL_TILE, m // tm),
        in_specs=[
            pl.BlockSpec((tm, k), lambda j, i: (i, 0)),
            pl.BlockSpec((None, k, COL_TILE), lambda j, i: (layer, 0, j)),
            pl.BlockSpec((tm, COL_TILE), lambda j, i: (i, j)),
        ],
        out_specs=pl.BlockSpec((tm, COL_TILE), lambda j, i: (i, j)),
        out_shape=jax.ShapeDtypeStruct((m, n), F32),
        compiler_params=_params(("parallel", "parallel"), 56),
        name="proj_res",
    )(a, w, x)


def _band_attn_kernel(q_ref, k0_ref, k1_ref, k2_ref, v0_ref, v1_ref, v2_ref, bias_ref, o_ref):
    k_refs = (k0_ref, k1_ref, k2_ref)
    v_refs = (v0_ref, v1_ref, v2_ref)
    for h in range(N_HEADS):
        cols = slice(h * HEAD_DIM, (h + 1) * HEAD_DIM)
        qh = q_ref[:, cols]
        scores = [lax.dot_general(qh, k_refs[t][:, cols], NT_DIMS, preferred_element_type=F32)
                  + bias_ref[h, :, t * Q_BLOCK:(t + 1) * Q_BLOCK] for t in range(KV_BLOCKS)]
        row_max = jnp.max(functools.reduce(jnp.maximum, scores), axis=-1, keepdims=True)
        probs = [jnp.exp2(s - row_max) for s in scores]
        denom = jnp.sum(functools.reduce(jnp.add, probs), axis=-1, keepdims=True)
        out = functools.reduce(jnp.add, [
            jnp.dot(p.astype(BF16), v_refs[t][:, cols], preferred_element_type=F32)
            for t, p in enumerate(probs)])
        o_ref[:, cols] = (out / denom).astype(o_ref.dtype)


def _band_attn(qk, v, bias):
    b, l, _ = v.shape

    def kv_spec(col, t):
        return pl.BlockSpec((None, Q_BLOCK, D_MODEL),
                            lambda bb, j: (bb, jnp.maximum(j - (KV_BLOCKS - 1 - t), 0), col))

    return pl.pallas_call(
        _band_attn_kernel,
        grid=(b, l // Q_BLOCK),
        in_specs=[pl.BlockSpec((None, Q_BLOCK, D_MODEL), lambda bb, j: (bb, j, 0))]
        + [kv_spec(1, t) for t in range(KV_BLOCKS)]
        + [kv_spec(0, t) for t in range(KV_BLOCKS)]
        + [pl.BlockSpec((None,) + bias.shape[1:], lambda bb, j: (jnp.minimum(j, KV_BLOCKS - 1), 0, 0, 0))],
        out_specs=pl.BlockSpec((None, Q_BLOCK, D_MODEL), lambda bb, j: (bb, j, 0)),
        out_shape=jax.ShapeDtypeStruct((b, l, D_MODEL), BF16),
        compiler_params=_params(("parallel", "parallel"), 56),
        name="band_attn",
    )(qk, qk, qk, qk, v, v, v, bias)


def _step_attn_kernel(qk_ref, v_ref, kc_ref, vc_ref, bias_c_ref, bias_n_ref, o_ref):
    for h in range(N_HEADS):
        cols = slice(h * HEAD_DIM, (h + 1) * HEAD_DIM)
        qh = qk_ref[:, cols].astype(BF16)
        kn = qk_ref[:, D_MODEL + h * HEAD_DIM:D_MODEL + (h + 1) * HEAD_DIM].astype(BF16)
        vn = v_ref[:, cols].astype(BF16)
        kc = kc_ref[:, h, :].astype(BF16)
        vc = vc_ref[:, h, :].astype(BF16)
        s_c = lax.dot_general(qh, kc, NT_DIMS, preferred_element_type=F32) + bias_c_ref[h]
        s_n = lax.dot_general(qh, kn, NT_DIMS, preferred_element_type=F32) + bias_n_ref[h]
        row_max = jnp.maximum(jnp.max(s_c, axis=-1, keepdims=True), jnp.max(s_n, axis=-1, keepdims=True))
        p_c = jnp.exp2(s_c - row_max)
        p_n = jnp.exp2(s_n - row_max)
        denom = jnp.sum(p_c, axis=-1, keepdims=True) + jnp.sum(p_n, axis=-1, keepdims=True)
        out = (jnp.dot(p_c.astype(BF16), vc, preferred_element_type=F32)
               + jnp.dot(p_n.astype(BF16), vn, preferred_element_type=F32))
        o_ref[:, cols] = (out / denom).astype(o_ref.dtype)


def _step_attn(qk, v, k_cache, v_cache, layer, bias_c, bias_n):
    b, s, _ = v.shape
    w = k_cache.shape[2]
    cache_spec = pl.BlockSpec((None, None, w, N_HEADS, HEAD_DIM), lambda i: (layer, i, 0, 0, 0))
    return pl.pallas_call(
        _step_attn_kernel,
        grid=(b,),
        in_specs=[
            pl.BlockSpec((None, s, 2 * D_MODEL), lambda i: (i, 0, 0)),
            pl.BlockSpec((None, s, D_MODEL), lambda i: (i, 0, 0)),
            cache_spec,
            cache_spec,
            pl.BlockSpec(bias_c.shape, lambda i: (0, 0, 0)),
            pl.BlockSpec(bias_n.shape, lambda i: (0, 0, 0)),
        ],
        out_specs=pl.BlockSpec((None, s, D_MODEL), lambda i: (i, 0, 0)),
        out_shape=jax.ShapeDtypeStruct((b, s, D_MODEL), BF16),
        compiler_params=_params(("parallel",), 48),
        name="step_attn",
    )(qk, v, k_cache, v_cache, bias_c, bias_n)


def _lag_vector(table, rows, width, dist0):
    period = width + rows
    m = np.arange(period)
    lag = np.where(m < width, m, m - period)
    idx = np.clip(dist0 - lag, -REL_CLIP, REL_CLIP) + REL_CLIP
    return table[:, idx] * LOG2E


def _skewed_bias(table, rows, width, dist0):
    u = _lag_vector(table, rows, width, dist0)
    period = u.shape[1]
    return jnp.tile(u, (1, rows))[:, :rows * (period - 1)].reshape(-1, rows, period - 1)[:, :, :width]


def _band_bias_kernel(u_ref, o_ref):
    n_variants, rows, width = o_ref.shape
    skew = pltpu.roll(jnp.broadcast_to(u_ref[...], (rows, u_ref.shape[-1])), 0, 1, stride=1, stride_axis=0)
    skew = skew[:, :width]
    qi = lax.broadcasted_iota(jnp.int32, (rows, width), 0)
    kn = lax.broadcasted_iota(jnp.int32, (rows, width), 1)
    chunk_gap = qi // CHUNK + (n_variants - 1) * (rows // CHUNK) - kn // CHUNK
    in_band = (chunk_gap >= 0) & (chunk_gap <= LEFT_CHUNKS)
    for v in range(n_variants):
        visible = in_band & (kn // rows >= n_variants - 1 - v)
        o_ref[v] = jnp.where(visible, skew, NEG_INF)


def _band_bias(table):
    width = KV_BLOCKS * Q_BLOCK
    u = _lag_vector(table, Q_BLOCK, width, (KV_BLOCKS - 1) * Q_BLOCK).astype(F32)
    n_heads, period = u.shape
    return pl.pallas_call(
        _band_bias_kernel,
        grid=(n_heads,),
        in_specs=[pl.BlockSpec((None, 1, period), lambda h: (h, 0, 0))],
        out_specs=pl.BlockSpec((KV_BLOCKS, None, Q_BLOCK, width), lambda h: (0, h, 0, 0)),
        out_shape=jax.ShapeDtypeStruct((KV_BLOCKS, n_heads, Q_BLOCK, width), F32),
        compiler_params=_params(("parallel",), 32),
        name="band_bias",
    )(u.reshape(n_heads, 1, period))


def _step_bias(table, s, w):
    q_pos = PAST_LEN + np.arange(s)
    k_pos = np.concatenate([PAST_LEN - w + np.arange(w), q_pos])
    qch, kch = q_pos // CHUNK, k_pos // CHUNK
    mask = (kch[None, :] <= qch[:, None]) & (kch[None, :] >= qch[:, None] - LEFT_CHUNKS)
    bias_c = jnp.where(mask[None, :, :w], _skewed_bias(table, s, w, w), NEG_INF).astype(F32)
    bias_n = jnp.where(mask[None, :, w:], _skewed_bias(table, s, s, 0), NEG_INF).astype(F32)
    return bias_c, bias_n


def _ssd_scan_kernel(xs_ref, b_ref, c_ref, dt_ref, z_ref, h0_ref, a_ref, dskip_ref, gg_ref,
                     yn_ref, hout_ref, state_ref, y_ref, src_t_ref, w_t_ref, *, rows):
    lc = SCAN_CHUNK
    i = pl.program_id(1)

    @pl.when(i == 0)
    def _():
        state_ref[...] = h0_ref[...]

    row_id = lax.broadcasted_iota(jnp.int32, (lc, lc), 0)
    col_id = lax.broadcasted_iota(jnp.int32, (lc, lc), 1)
    causal = col_id <= row_id
    causal_f = causal.astype(F32)
    low_half = lax.broadcasted_iota(jnp.int32, (lc, PAIR), 1) < SSM_HEADDIM
    low_half_row = lax.broadcasted_iota(jnp.int32, (1, PAIR), 1) < SSM_HEADDIM

    def chunk(c, carry):
        rows_c = pl.ds(pl.multiple_of(c * lc, lc), lc)
        dt = dt_ref[rows_c, :]
        a_cs = jnp.dot(causal_f, dt * (a_ref[...] * LOG2E), precision=lax.Precision.HIGHEST,
                       preferred_element_type=F32)
        a_cs_t = a_cs.T
        dt_t = dt.T
        src_t_ref[...] = a_cs_t - jnp.log2(dt_t)
        w_t_ref[...] = dt_t * jnp.exp2(a_cs_t[:, lc - 1:lc] - a_cs_t)
        chunk_decay = jnp.exp2(a_cs[lc - 1:lc, :])
        for g in range(N_GROUPS):
            gcols = slice(g * D_STATE, (g + 1) * D_STATE)
            bg = b_ref[rows_c, gcols]
            cg = c_ref[rows_c, gcols]
            cb = lax.dot_general(cg, bg, NT_DIMS, preferred_element_type=F32)
            bg_t = bg.astype(F32).T
            cg_f = cg.astype(F32)
            for jp in range(PAIRS_PER_GROUP):
                q = g * PAIRS_PER_GROUP + jp
                pcols = slice(q * PAIR, (q + 1) * PAIR)
                xp = xs_ref[rows_c, pcols]
                h_t = state_ref[q]
                lhs_y, lhs_s = [], []
                for h in (2 * q, 2 * q + 1):
                    col = jnp.broadcast_to(a_cs[:, h:h + 1], (lc, lc))
                    row = jnp.broadcast_to(src_t_ref[h:h + 1, :], (lc, lc))
                    within = cb * jnp.exp2(jnp.where(causal, col - row, -jnp.inf))
                    carried = cg_f * jnp.exp2(col)
                    lhs_y.append(jnp.concatenate([within.astype(BF16), carried.astype(BF16)], axis=1))
                    lhs_s.append((bg_t * jnp.broadcast_to(w_t_ref[h:h + 1, :], (D_STATE, lc))).astype(BF16))
                y2 = jnp.dot(jnp.concatenate(lhs_y, axis=0), jnp.concatenate([xp, h_t.astype(BF16)], axis=0),
                             preferred_element_type=F32)
                s2 = jnp.dot(jnp.concatenate(lhs_s, axis=0), xp, preferred_element_type=F32)
                decay = jnp.where(low_half_row,
                                  jnp.broadcast_to(chunk_decay[:, 2 * q:2 * q + 1], (1, PAIR)),
                                  jnp.broadcast_to(chunk_decay[:, 2 * q + 1:2 * q + 2], (1, PAIR)))
                state_ref[q] = h_t * decay + jnp.where(low_half, s2[:D_STATE], s2[D_STATE:])
                y_ref[rows_c, pcols] = (jnp.where(low_half, y2[:lc], y2[lc:])
                                        + dskip_ref[:, pcols] * xp.astype(F32))
        return carry

    lax.fori_loop(0, rows // lc, chunk, 0)

    y = y_ref[...] * _silu(z_ref[...].astype(F32))
    yn_ref[...] = (y * _rms_scale(y) * gg_ref[...]).astype(yn_ref.dtype)

    @pl.when(i == pl.num_programs(1) - 1)
    def _():
        hout_ref[...] = state_ref[...]


def _ssd_scan(xbc, dt, z, h0, a_neg, d_skip, gate_g, *, rows):
    b, l, _ = xbc.shape
    gn = N_GROUPS * D_STATE
    b_block = D_INNER // gn
    return pl.pallas_call(
        functools.partial(_ssd_scan_kernel, rows=rows),
        grid=(b, l // rows),
        in_specs=[
            pl.BlockSpec((None, rows, D_INNER), lambda bb, i: (bb, i, 0)),
            pl.BlockSpec((None, rows, gn), lambda bb, i: (bb, i, b_block)),
            pl.BlockSpec((None, rows, gn), lambda bb, i: (bb, i, b_block + 1)),
            pl.BlockSpec((None, rows, LANES), lambda bb, i: (bb, i, 0)),
            pl.BlockSpec((None, rows, D_INNER), lambda bb, i: (bb, i, 0)),
            pl.BlockSpec((None, N_PAIRS, D_STATE, PAIR), lambda bb, i: (bb, 0, 0, 0)),
            pl.BlockSpec((1, LANES), lambda bb, i: (0, 0)),
            pl.BlockSpec((1, D_INNER), lambda bb, i: (0, 0)),
            pl.BlockSpec((1, D_INNER), lambda bb, i: (0, 0)),
        ],
        out_specs=[
            pl.BlockSpec((None, rows, D_INNER), lambda bb, i: (bb, i, 0)),
            pl.BlockSpec((None, N_PAIRS, D_STATE, PAIR), lambda bb, i: (bb, 0, 0, 0)),
        ],
        out_shape=[
            jax.ShapeDtypeStruct((b, l, D_INNER), BF16),
            jax.ShapeDtypeStruct((b, N_PAIRS, D_STATE, PAIR), F32),
        ],
        scratch_shapes=[
            pltpu.VMEM((N_PAIRS, D_STATE, PAIR), F32),
            pltpu.VMEM((rows, D_INNER), F32),
            pltpu.VMEM((LANES, SCAN_CHUNK), F32),
            pltpu.VMEM((LANES, SCAN_CHUNK), F32),
        ],
        compiler_params=_params(("arbitrary", "arbitrary"), 48),
        name="ssd_scan",
    )(xbc, xbc, xbc, dt, z, h0, a_neg, d_skip, gate_g)


def _to_pair_layout(h):
    b = h.shape[0]
    return h.reshape(b, N_PAIRS, 2, SSM_HEADDIM, D_STATE).transpose(0, 1, 4, 2, 3).reshape(b, N_PAIRS, D_STATE, PAIR)


def _from_pair_layout(h):
    b = h.shape[0]
    return (h.reshape(b, N_PAIRS, D_STATE, 2, SSM_HEADDIM).transpose(0, 1, 3, 4, 2)
            .reshape(b, N_SSM_HEADS, SSM_HEADDIM, D_STATE))


def _ffn_layer(xp, xs, g, w_gu, w_d, layer, half):
    xs, w_g, w_u, w_dn = _ffn_cast(xs, g, w_gu, w_d, layer, half, tf=512)
    xp = _ffn(xp, g, w_g, w_u, w_dn, tm=1024, tf=512, sub=MXU_DIM)
    return xp, xs


def _attn_layer(xp, xs, cache_k, cache_v, g, w_qkv, w_o, layer, q_gain, k_gain, table, *, batch, seq):
    dec_batch, w = cache_k.shape[1], cache_k.shape[2]
    dec_seq = xs.shape[0] // dec_batch
    gains = jnp.stack([q_gain * (HEAD_DIM ** -0.5 * LOG2E), k_gain]).astype(F32)
    m_s = dec_batch * dec_seq
    qk_s, v_s, w_qk, w_v = _qkv(xs, g, w_qkv, w_qkv, layer, 2 * D_MODEL, gains, tm=m_s, out_dtype=F32,
                                emit_bf16=True)
    w_qk, w_v = w_qk[None], w_v[None]
    ks_new = qk_s[:, D_MODEL:].reshape(dec_batch, dec_seq, N_HEADS, HEAD_DIM)
    vs_new = v_s.reshape(dec_batch, dec_seq, N_HEADS, HEAD_DIM)
    bias_c, bias_n = _step_bias(table, dec_seq, w)
    o_s = _step_attn(qk_s.reshape(dec_batch, dec_seq, 2 * D_MODEL), v_s.reshape(dec_batch, dec_seq, D_MODEL),
                     cache_k, cache_v, layer, bias_c, bias_n)
    xs = _proj_res(o_s.reshape(m_s, D_MODEL), w_o, layer, xs, tm=m_s)
    qk, v = _qkv(xp, g, w_qk, w_v, 0, 0, gains, tm=1024, out_dtype=BF16)
    win = min(LEFT_CHUNKS * CHUNK, seq)
    x_tail = xp.reshape(batch, seq, D_MODEL)[:, seq - win:].reshape(batch * win, D_MODEL)
    qk_tail, v_tail = _qkv(x_tail, g, w_qk, w_v, 0, 0, gains, tm=batch * win, out_dtype=F32)
    k_new = qk_tail[:, D_MODEL:].reshape(batch, win, N_HEADS, HEAD_DIM)
    v_new = v_tail.reshape(batch, win, N_HEADS, HEAD_DIM)
    o = _band_attn(qk.reshape(batch, seq, 2 * D_MODEL), v.reshape(batch, seq, D_MODEL), _band_bias(table))
    xp = _proj_res(o.reshape(batch * seq, D_MODEL), w_o, layer, xp, tm=1024)
    return xp, xs, k_new, v_new, ks_new, vs_new


def _ssd_stream(x2d, h0_pairs, conv_init, p, *, batch, seq, conv_batch, tm, rows):
    g, w_in, w_out, layer, w_dt, conv_w, conv_b, dt_bias, a_neg, d_skip, gate_g = p
    row_tile = min(1024, batch * seq)
    z, dt = _ssd_z_dt(x2d, g, w_in, layer, w_dt, dt_bias, tm=row_tile)
    xbc, tails = _ssd_xbc(x2d.reshape(conv_batch, -1, D_MODEL), g, w_in, layer, conv_w, conv_b, conv_init, tm=tm)
    xbc = xbc.reshape(batch, seq, CONV_DIM)
    z = z.reshape(batch, seq, D_INNER)
    dt = dt.reshape(batch, seq, LANES)
    pad = (-seq) % SCAN_CHUNK
    if pad:
        widen = lambda t: jnp.pad(t, ((0, 0), (0, pad), (0, 0)))
        xbc, z, dt = widen(xbc), widen(z), widen(dt)
    yn, h_new = _ssd_scan(xbc, dt, z, h0_pairs, a_neg, d_skip, gate_g, rows=rows)
    yn = yn[:, :seq].reshape(batch * seq, D_INNER)
    x2d = _proj_res(yn, w_out, layer, x2d, tm=row_tile)
    conv_state = tails[:, -1, :, SUBLANES - (CONV_WIDTH - 1):].reshape(batch, CONV_WIDTH - 1, CONV_DIM)
    return x2d, _from_pair_layout(h_new), conv_state


def _ssd_layer(xp, xs, state_ssm, state_conv, g, w_in, w_out, layer, w_dt, conv_w, conv_b, dt_bias, a_log,
               d_skip, gate_g, *, batch, seq):
    dec_batch = state_ssm.shape[0]
    dec_seq = xs.shape[0] // dec_batch
    lane_pad = (0, LANES - N_SSM_HEADS)
    p = (g, w_in, w_out, layer, w_dt, conv_w, conv_b,
         jnp.pad(dt_bias.astype(F32), lane_pad).reshape(1, LANES),
         jnp.pad(-jnp.exp(a_log.astype(F32)), lane_pad).reshape(1, LANES),
         jnp.repeat(d_skip, SSM_HEADDIM).reshape(1, D_INNER),
         gate_g.reshape(1, D_INNER))
    conv_init = jnp.pad(state_conv, ((0, 0), (SUBLANES - (CONV_WIDTH - 1), 0), (0, 0)))[None]
    xs, hs, cs = _ssd_stream(xs, _to_pair_layout(state_ssm), conv_init, p, batch=dec_batch, seq=dec_seq,
                             conv_batch=1, tm=dec_batch * dec_seq, rows=SCAN_CHUNK)
    zero_state = jnp.zeros((batch, N_PAIRS, D_STATE, PAIR), F32)
    zero_conv = jnp.zeros((batch, 1, SUBLANES, CONV_DIM), F32)
    xp, hp, cp = _ssd_stream(xp, zero_state, zero_conv, p, batch=batch, seq=seq, conv_batch=batch, tm=1024,
                             rows=2 * SCAN_CHUNK)
    return xp, xs, hp, cp, hs, cs


def kernel(x_prompt, x_sample, cache_k, cache_v, state_ssm, state_conv, ffn_norm, ffn_w_gate_up, ffn_w_down, attn_norm, attn_w_qkv, attn_q_norm, attn_k_norm, attn_rel_bias, attn_w_o, ssd_norm, ssd_w_in, ssd_conv_w, ssd_conv_b, ssd_dt_bias, ssd_a_log, ssd_d_skip, ssd_gate_norm, ssd_w_out):
    batch, seq, _ = x_prompt.shape
    dec_batch, dec_seq, _ = x_sample.shape
    depth = ffn_norm.shape[0]
    xp = x_prompt.reshape(batch * seq, D_MODEL)
    xs = x_sample.reshape(dec_batch * dec_seq, D_MODEL)

    w_o = attn_w_o.astype(BF16)
    w_in = ssd_w_in.astype(BF16)
    w_out = ssd_w_out.astype(BF16)
    w_dt = jnp.pad(w_in[:, :, D_INNER + CONV_DIM:], ((0, 0), (0, 0), (0, LANES - N_SSM_HEADS)))

    kp, vp, hp, cp, ks, vs, hs, cs = [], [], [], [], [], [], [], []
    for i in range(depth):
        j = i // 2
        xp, xs = _ffn_layer(xp, xs, ffn_norm[i, 0], ffn_w_gate_up, ffn_w_down, i, 0)
        if i % 2 == 0:
            xp, xs, k_new, v_new, ks_new, vs_new = _attn_layer(
                xp, xs, cache_k, cache_v, attn_norm[j], attn_w_qkv, w_o, j, attn_q_norm[j], attn_k_norm[j],
                attn_rel_bias[j], batch=batch, seq=seq)
            kp.append(k_new); vp.append(v_new); ks.append(ks_new); vs.append(vs_new)
        else:
            xp, xs, h_p, c_p, h_s, c_s = _ssd_layer(
                xp, xs, state_ssm[j], state_conv[j], ssd_norm[j], w_in, w_out, j,
                w_dt, ssd_conv_w[j], ssd_conv_b[j], ssd_dt_bias[j], ssd_a_log[j],
                ssd_d_skip[j], ssd_gate_norm[j], batch=batch, seq=seq)
            hp.append(h_p); cp.append(c_p); hs.append(h_s); cs.append(c_s)
        xp, xs = _ffn_layer(xp, xs, ffn_norm[i, 1], ffn_w_gate_up, ffn_w_down, i, 1)

    return (xp.reshape(batch, seq, D_MODEL), xs.reshape(dec_batch, dec_seq, D_MODEL),
            jnp.stack(kp), jnp.stack(vp), jnp.stack(hp), jnp.stack(cp),
            jnp.stack(ks), jnp.stack(vs), jnp.stack(hs), jnp.stack(cs))
```

```python
import functools
import math

import jax
import jax.numpy as jnp
import numpy as np
from jax import lax
from jax.experimental import pallas as pl
from jax.experimental.pallas import tpu as pltpu

F32 = jnp.float32
BF16 = jnp.bfloat16

D_MODEL = 2048
CHUNK = 64
LEFT_CHUNKS = 8
HEAD_DIM = 128
N_HEADS = D_MODEL // HEAD_DIM
REL_CLIP = 128
PAST_LEN = 1024
D_INNER = 2 * D_MODEL
SSM_HEADDIM = 64
N_SSM_HEADS = D_INNER // SSM_HEADDIM
N_GROUPS = 8
D_STATE = 128
CONV_WIDTH = 4
CONV_DIM = D_INNER + 2 * N_GROUPS * D_STATE
EPS = 1e-6
NEG_INF = -1e30
LOG2E = math.log2(math.e)

LANES = 128
SUBLANES = 8
MXU_DIM = 256
MIB = 1024 * 1024

Q_BLOCK = 4 * CHUNK
KV_BLOCKS = LEFT_CHUNKS * CHUNK // Q_BLOCK + 1
SCAN_CHUNK = 128
PAIR = 2 * SSM_HEADDIM
N_PAIRS = N_SSM_HEADS // 2
PAIRS_PER_GROUP = N_PAIRS // N_GROUPS
COL_TILE = 1024
assert SCAN_CHUNK == D_STATE == PAIR == LANES

NT_DIMS = (((1,), (1,)), ((), ()))


def _params(semantics, vmem_mib):
    return pltpu.CompilerParams(dimension_semantics=semantics, vmem_limit_bytes=vmem_mib * MIB)


def _rms_scale(x):
    return lax.rsqrt(jnp.mean(x * x, axis=-1, keepdims=True) + EPS)


def _silu(x):
    return x * jax.nn.sigmoid(x)


def _with_norm_on_first(x_ref, g_ref, xn_ref, first, work):
    @pl.when(first)
    def _():
        x = x_ref[...]
        xn_ref[...] = (x * _rms_scale(x) * g_ref[...]).astype(BF16)
        work()

    @pl.when(jnp.logical_not(first))
    def _():
        work()


def _ffn_cast_kernel(x_ref, g_ref, wg_ref, wu_ref, wd_ref, o_ref, wgb_ref, wub_ref, wdb_ref, xn_ref):
    @pl.when(pl.program_id(0) == 0)
    def _():
        x = x_ref[...]
        xn_ref[...] = (x * _rms_scale(x) * g_ref[...]).astype(BF16)
        o_ref[...] = x

    wg = wg_ref[...].astype(BF16)
    wu = wu_ref[...].astype(BF16)
    wd = wd_ref[...].astype(BF16)
    wgb_ref[...] = wg
    wub_ref[...] = wu
    wdb_ref[...] = wd
    xn = xn_ref[...]
    gate = jnp.dot(xn, wg, preferred_element_type=F32)
    up = jnp.dot(xn, wu, preferred_element_type=F32)
    act = (_silu(gate) * up).astype(BF16)
    o_ref[...] += 0.5 * jnp.dot(act, wd, preferred_element_type=F32)


def _ffn_cast(x, g, w_gu, w_d, layer, half, *, tf):
    m, d = x.shape
    d_ff = w_d.shape[2]
    nf = d_ff // tf
    return pl.pallas_call(
        _ffn_cast_kernel,
        grid=(nf,),
        in_specs=[
            pl.BlockSpec((m, d), lambda f: (0, 0)),
            pl.BlockSpec((1, d), lambda f: (0, 0)),
            pl.BlockSpec((None, None, d, tf), lambda f: (layer, half, 0, f)),
            pl.BlockSpec((None, None, d, tf), lambda f: (layer, half, 0, f + nf)),
            pl.BlockSpec((None, None, tf, d), lambda f: (layer, half, f, 0)),
        ],
        out_specs=[
            pl.BlockSpec((m, d), lambda f: (0, 0)),
            pl.BlockSpec((d, tf), lambda f: (0, f)),
            pl.BlockSpec((d, tf), lambda f: (0, f)),
            pl.BlockSpec((tf, d), lambda f: (f, 0)),
        ],
        out_shape=[
            jax.ShapeDtypeStruct((m, d), F32),
            jax.ShapeDtypeStruct((d, d_ff), BF16),
            jax.ShapeDtypeStruct((d, d_ff), BF16),
            jax.ShapeDtypeStruct((d_ff, d), BF16),
        ],
        scratch_shapes=[pltpu.VMEM((m, d), BF16)],
        compiler_params=_params(("arbitrary",), 56),
        name="ffn_cast",
    )(x, g.reshape(1, d), w_gu, w_gu, w_d)


def _ffn_kernel(x_ref, g_ref, wg_ref, wu_ref, wd_ref, o_ref, xn_ref, *, sub):
    def half_down():
        xn = xn_ref[...]
        down = None
        for c in range(wg_ref.shape[1] // sub):
            cs = slice(c * sub, (c + 1) * sub)
            gate = jnp.dot(xn, wg_ref[:, cs], preferred_element_type=F32)
            up = jnp.dot(xn, wu_ref[:, cs], preferred_element_type=F32)
            act = (_silu(gate) * up).astype(BF16)
            part = jnp.dot(act, wd_ref[cs, :], preferred_element_type=F32)
            down = part if down is None else down + part
        return 0.5 * down

    first = pl.program_id(1) == 0

    @pl.when(first)
    def _():
        x = x_ref[...]
        xn_ref[...] = (x * _rms_scale(x) * g_ref[...]).astype(BF16)
        o_ref[...] = x_ref[...] + half_down()

    @pl.when(jnp.logical_not(first))
    def _():
        o_ref[...] += half_down()


def _ffn(x, g, w_g, w_u, w_d, *, tm, tf, sub):
    m, d = x.shape
    d_ff = w_d.shape[0]
    return pl.pallas_call(
        functools.partial(_ffn_kernel, sub=sub),
        grid=(m // tm, d_ff // tf),
        in_specs=[
            pl.BlockSpec((tm, d), lambda i, f: (i, 0)),
            pl.BlockSpec((1, d), lambda i, f: (0, 0)),
            pl.BlockSpec((d, tf), lambda i, f: (0, f)),
            pl.BlockSpec((d, tf), lambda i, f: (0, f)),
            pl.BlockSpec((tf, d), lambda i, f: (f, 0)),
        ],
        out_specs=pl.BlockSpec((tm, d), lambda i, f: (i, 0)),
        out_shape=jax.ShapeDtypeStruct((m, d), F32),
        scratch_shapes=[pltpu.VMEM((tm, d), BF16)],
        compiler_params=_params(("parallel", "arbitrary"), 58),
        name="ffn",
    )(x, g.reshape(1, d), w_g, w_u, w_d)


def _qkv_kernel(x_ref, g_ref, wqk_ref, wv_ref, gain_ref, qk_ref, v_ref, *rest, sub, emit_bf16):
    xn_ref = rest[-1]
    n = pl.program_id(1)

    def work():
        if emit_bf16:
            wqk, wv = rest[0], rest[1]
            wqk[...] = wqk_ref[...].astype(BF16)
            wv[...] = wv_ref[...].astype(BF16)
        else:
            wqk, wv = wqk_ref, wv_ref
        xn = xn_ref[...]
        gain = gain_ref[pl.ds(n // (pl.num_programs(1) // 2), 1), :]
        for c in range(wqk_ref.shape[1] // sub):
            acc = jnp.dot(xn, wqk[:, c * sub:(c + 1) * sub], preferred_element_type=F32)
            for h in range(sub // HEAD_DIM):
                a = acc[:, h * HEAD_DIM:(h + 1) * HEAD_DIM]
                cols = slice(c * sub + h * HEAD_DIM, c * sub + (h + 1) * HEAD_DIM)
                qk_ref[:, cols] = (a * _rms_scale(a) * gain).astype(qk_ref.dtype)
        v_ref[...] = jnp.dot(xn, wv[...], preferred_element_type=F32).astype(v_ref.dtype)

    _with_norm_on_first(x_ref, g_ref, xn_ref, n == 0, work)


def _qkv(x, g, w_qk, w_v, layer, v_col0, gains, *, tm, out_dtype, emit_bf16=False):
    m, d = x.shape
    n_tiles = 2 * D_MODEL // COL_TILE
    tv = D_MODEL // n_tiles
    v0 = v_col0 // tv
    assert not emit_bf16 or m == tm
    out_specs = [
        pl.BlockSpec((tm, COL_TILE), lambda i, n: (i, n)),
        pl.BlockSpec((tm, tv), lambda i, n: (i, n)),
    ]
    out_shape = [
        jax.ShapeDtypeStruct((m, 2 * D_MODEL), out_dtype),
        jax.ShapeDtypeStruct((m, D_MODEL), out_dtype),
    ]
    if emit_bf16:
        out_specs += [pl.BlockSpec((d, COL_TILE), lambda i, n: (0, n)), pl.BlockSpec((d, tv), lambda i, n: (0, n))]
        out_shape += [jax.ShapeDtypeStruct((d, 2 * D_MODEL), BF16), jax.ShapeDtypeStruct((d, D_MODEL), BF16)]
    return pl.pallas_call(
        functools.partial(_qkv_kernel, sub=MXU_DIM, emit_bf16=emit_bf16),
        grid=(m // tm, n_tiles),
        in_specs=[
            pl.BlockSpec((tm, d), lambda i, n: (i, 0)),
            pl.BlockSpec((1, d), lambda i, n: (0, 0)),
            pl.BlockSpec((None, d, COL_TILE), lambda i, n: (layer, 0, n)),
            pl.BlockSpec((None, d, tv), lambda i, n: (layer, 0, v0 + n)),
            pl.BlockSpec((2, HEAD_DIM), lambda i, n: (0, 0)),
        ],
        out_specs=out_specs,
        out_shape=out_shape,
        scratch_shapes=[pltpu.VMEM((tm, d), BF16)],
        compiler_params=_params(("parallel", "arbitrary"), 48),
        name="qkv",
    )(x, g.reshape(1, d), w_qk, w_v, gains)


def _z_dt_kernel(x_ref, g_ref, w_ref, wdt_ref, b_ref, z_ref, dt_ref, xn_ref):
    def project():
        z_ref[...] = jnp.dot(xn_ref[...], w_ref[...], preferred_element_type=F32).astype(z_ref.dtype)

    def project_with_dt():
        raw = jnp.dot(xn_ref[...], wdt_ref[...], preferred_element_type=F32) + b_ref[...]
        dt_ref[...] = jnp.maximum(raw, 0.0) + jnp.log1p(jnp.exp(-jnp.abs(raw)))
        project()

    first = pl.program_id(1) == 0

    @pl.when(first)
    def _():
        x = x_ref[...]
        xn_ref[...] = (x * _rms_scale(x) * g_ref[...]).astype(BF16)
        project_with_dt()

    @pl.when(jnp.logical_not(first))
    def _():
        project()


def _ssd_z_dt(x, g, w_in, layer, w_dt, dt_bias, *, tm):
    m, d = x.shape
    tn = 2 * COL_TILE
    return pl.pallas_call(
        _z_dt_kernel,
        grid=(m // tm, D_INNER // tn),
        in_specs=[
            pl.BlockSpec((tm, d), lambda i, j: (i, 0)),
            pl.BlockSpec((1, d), lambda i, j: (0, 0)),
            pl.BlockSpec((None, d, tn), lambda i, j: (layer, 0, j)),
            pl.BlockSpec((None, d, LANES), lambda i, j: (layer, 0, 0)),
            pl.BlockSpec((1, LANES), lambda i, j: (0, 0)),
        ],
        out_specs=[
            pl.BlockSpec((tm, tn), lambda i, j: (i, j)),
            pl.BlockSpec((tm, LANES), lambda i, j: (i, 0)),
        ],
        out_shape=[
            jax.ShapeDtypeStruct((m, D_INNER), BF16),
            jax.ShapeDtypeStruct((m, LANES), F32),
        ],
        scratch_shapes=[pltpu.VMEM((tm, d), BF16)],
        compiler_params=_params(("parallel", "arbitrary"), 56),
        name="ssd_z_dt",
    )(x, g.reshape(1, d), w_in, w_dt, dt_bias)


def _xbc_kernel(x_ref, g_ref, w_ref, cw_ref, cb_ref, init_ref, o_ref, st_ref, xn_ref, carry_ref, *, tm, sub,
                row_block, streams):
    i = pl.program_id(1)
    n = pl.program_id(2)
    chained = streams == 1

    if chained:
        @pl.when(i == 0)
        def _():
            carry_ref[n] = init_ref[0]

    rb = row_block
    sublane_id = lax.broadcasted_iota(jnp.int32, (1, SUBLANES, sub), 1)

    def work():
        for c in range(w_ref.shape[1] // sub):
            cs = slice(c * sub, (c + 1) * sub)
            taps = [cw_ref[t:t + 1, cs] for t in range(CONV_WIDTH)]
            carry = carry_ref[n, :, cs] if chained else None
            for r in range(tm // rb):
                rows = slice(r * rb, (r + 1) * rb)
                if not chained:
                    carry = init_ref[r, :, cs]
                raw = jnp.dot(xn_ref[rows, :], w_ref[:, cs], preferred_element_type=F32)
                tiles = jnp.concatenate([carry, raw], axis=0).reshape(rb // SUBLANES + 1, SUBLANES, sub)
                conv = cb_ref[:, cs] + raw * taps[CONV_WIDTH - 1]
                conv = conv.reshape(rb // SUBLANES, SUBLANES, sub)
                for t in range(CONV_WIDTH - 1):
                    lag = CONV_WIDTH - 1 - t
                    rot = pltpu.roll(tiles, lag, 1)
                    conv = conv + jnp.where(sublane_id < lag, rot[:-1], rot[1:]) * taps[t]
                o_ref[rows, cs] = _silu(conv).reshape(rb, sub).astype(o_ref.dtype)
                carry = raw[rb - SUBLANES:rb]
                if not chained:
                    st_ref[r, :, cs] = carry
            if chained:
                carry_ref[n, :, cs] = carry
                st_ref[0, :, cs] = carry

    _with_norm_on_first(x_ref, g_ref, xn_ref, n == 0, work)


def _ssd_xbc(x, g, w_in, layer, conv_w, conv_b, init, *, tm):
    b, l, d = x.shape
    streams = init.shape[1]
    n_tiles = CONV_DIM // COL_TILE
    tile0 = D_INNER // COL_TILE
    assert streams == 1 or l == tm
    rb = min(tm, 512) if streams == 1 else tm // streams
    out_specs = [
        pl.BlockSpec((None, tm, COL_TILE), lambda bb, i, n: (bb, i, n)),
        pl.BlockSpec((None, None, streams, SUBLANES, COL_TILE), lambda bb, i, n: (bb, i, 0, 0, n)),
    ]
    out_shape = [
        jax.ShapeDtypeStruct((b, l, CONV_DIM), BF16),
        jax.ShapeDtypeStruct((b, l // tm, streams, SUBLANES, CONV_DIM), F32),
    ]
    return pl.pallas_call(
        functools.partial(_xbc_kernel, tm=tm, sub=MXU_DIM, row_block=rb, streams=streams),
        grid=(b, l // tm, n_tiles),
        in_specs=[
            pl.BlockSpec((None, tm, d), lambda bb, i, n: (bb, i, 0)),
            pl.BlockSpec((1, d), lambda bb, i, n: (0, 0)),
            pl.BlockSpec((None, d, COL_TILE), lambda bb, i, n: (layer, 0, tile0 + n)),
            pl.BlockSpec((CONV_WIDTH, COL_TILE), lambda bb, i, n: (0, n)),
            pl.BlockSpec((1, COL_TILE), lambda bb, i, n: (0, n)),
            pl.BlockSpec((None, streams, SUBLANES, COL_TILE), lambda bb, i, n: (bb, 0, 0, n)),
        ],
        out_specs=out_specs,
        out_shape=out_shape,
        scratch_shapes=[
            pltpu.VMEM((tm, d), BF16),
            pltpu.VMEM((n_tiles, SUBLANES, COL_TILE), F32),
        ],
        compiler_params=_params(("arbitrary", "arbitrary", "arbitrary"), 48),
        name="ssd_xbc",
    )(x, g.reshape(1, d), w_in, conv_w, conv_b.reshape(1, CONV_DIM), init)


def _proj_res_kernel(a_ref, w_ref, x_ref, o_ref):
    o_ref[...] = x_ref[...] + jnp.dot(a_ref[...], w_ref[...], preferred_element_type=F32)


def _proj_res(a, w, layer, x, *, tm):
    m, k = a.shape
    n = w.shape[2]
    return pl.pallas_call(
        _proj_res_kernel,
        grid=(n // COL_TILE, m // tm),
        in_specs=[
            pl.BlockSpec((tm, k), lambda j, i: (i, 0)),
            pl.BlockSpec((None, k, COL_TILE), lambda j, i: (layer, 0, j)),
            pl.BlockSpec((tm, COL_TILE), lambda j, i: (i, j)),
        ],
        out_specs=pl.BlockSpec((tm, COL_TILE), lambda j, i: (i, j)),
        out_shape=jax.ShapeDtypeStruct((m, n), F32),
        compiler_params=_params(("parallel", "parallel"), 56),
        name="proj_res",
    )(a, w, x)


def _band_attn_kernel(q_ref, k0_ref, k1_ref, k2_ref, v0_ref, v1_ref, v2_ref, bias_ref, o_ref):
    k_refs = (k0_ref, k1_ref, k2_ref)
    v_refs = (v0_ref, v1_ref, v2_ref)
    for h in range(N_HEADS):
        cols = slice(h * HEAD_DIM, (h + 1) * HEAD_DIM)
        qh = q_ref[:, cols]
        scores = [lax.dot_general(qh, k_refs[t][:, cols], NT_DIMS, preferred_element_type=F32)
                  + bias_ref[h, :, t * Q_BLOCK:(t + 1) * Q_BLOCK] for t in range(KV_BLOCKS)]
        row_max = jnp.max(functools.reduce(jnp.maximum, scores), axis=-1, keepdims=True)
        probs = [jnp.exp2(s - row_max) for s in scores]
        denom = jnp.sum(functools.reduce(jnp.add, probs), axis=-1, keepdims=True)
        out = functools.reduce(jnp.add, [
            jnp.dot(p.astype(BF16), v_refs[t][:, cols], preferred_element_type=F32)
            for t, p in enumerate(probs)])
        o_ref[:, cols] = (out / denom).astype(o_ref.dtype)


def _band_attn(qk, v, bias):
    b, l, _ = v.shape

    def kv_spec(col, t):
        return pl.BlockSpec((None, Q_BLOCK, D_MODEL),
                            lambda bb, j: (bb, jnp.maximum(j - (KV_BLOCKS - 1 - t), 0), col))

    return pl.pallas_call(
        _band_attn_kernel,
        grid=(b, l // Q_BLOCK),
        in_specs=[pl.BlockSpec((None, Q_BLOCK, D_MODEL), lambda bb, j: (bb, j, 0))]
        + [kv_spec(1, t) for t in range(KV_BLOCKS)]
        + [kv_spec(0, t) for t in range(KV_BLOCKS)]
        + [pl.BlockSpec((None,) + bias.shape[1:], lambda bb, j: (jnp.minimum(j, KV_BLOCKS - 1), 0, 0, 0))],
        out_specs=pl.BlockSpec((None, Q_BLOCK, D_MODEL), lambda bb, j: (bb, j, 0)),
        out_shape=jax.ShapeDtypeStruct((b, l, D_MODEL), BF16),
        compiler_params=_params(("parallel", "parallel"), 56),
        name="band_attn",
    )(qk, qk, qk, qk, v, v, v, bias)


def _step_attn_kernel(qk_ref, v_ref, kc_ref, vc_ref, bias_c_ref, bias_n_ref, o_ref):
    for h in range(N_HEADS):
        cols = slice(h * HEAD_DIM, (h + 1) * HEAD_DIM)
        qh = qk_ref[:, cols].astype(BF16)
        kn = qk_ref[:, D_MODEL + h * HEAD_DIM:D_MODEL + (h + 1) * HEAD_DIM].astype(BF16)
        vn = v_ref[:, cols].astype(BF16)
        kc = kc_ref[:, h, :].astype(BF16)
        vc = vc_ref[:, h, :].astype(BF16)
        s_c = lax.dot_general(qh, kc, NT_DIMS, preferred_element_type=F32) + bias_c_ref[h]
        s_n = lax.dot_general(qh, kn, NT_DIMS, preferred_element_type=F32) + bias_n_ref[h]
        row_max = jnp.maximum(jnp.max(s_c, axis=-1, keepdims=True), jnp.max(s_n, axis=-1, keepdims=True))
        p_c = jnp.exp2(s_c - row_max)
        p_n = jnp.exp2(s_n - row_max)
        denom = jnp.sum(p_c, axis=-1, keepdims=True) + jnp.sum(p_n, axis=-1, keepdims=True)
        out = (jnp.dot(p_c.astype(BF16), vc, preferred_element_type=F32)
               + jnp.dot(p_n.astype(BF16), vn, preferred_element_type=F32))
        o_ref[:, cols] = (out / denom).astype(o_ref.dtype)


def _step_attn(qk, v, k_cache, v_cache, layer, bias_c, bias_n):
    b, s, _ = v.shape
    w = k_cache.shape[2]
    cache_spec = pl.BlockSpec((None, None, w, N_HEADS, HEAD_DIM), lambda i: (layer, i, 0, 0, 0))
    return pl.pallas_call(
        _step_attn_kernel,
        grid=(b,),
        in_specs=[
            pl.BlockSpec((None, s, 2 * D_MODEL), lambda i: (i, 0, 0)),
            pl.BlockSpec((None, s, D_MODEL), lambda i: (i, 0, 0)),
            cache_spec,
            cache_spec,
            pl.BlockSpec(bias_c.shape, lambda i: (0, 0, 0)),
            pl.BlockSpec(bias_n.shape, lambda i: (0, 0, 0)),
        ],
        out_specs=pl.BlockSpec((None, s, D_MODEL), lambda i: (i, 0, 0)),
        out_shape=jax.ShapeDtypeStruct((b, s, D_MODEL), BF16),
        compiler_params=_params(("parallel",), 48),
        name="step_attn",
    )(qk, v, k_cache, v_cache, bias_c, bias_n)


def _lag_vector(table, rows, width, dist0):
    period = width + rows
    m = np.arange(period)
    lag = np.where(m < width, m, m - period)
    idx = np.clip(dist0 - lag, -REL_CLIP, REL_CLIP) + REL_CLIP
    return table[:, idx] * LOG2E


def _skewed_bias(table, rows, width, dist0):
    u = _lag_vector(table, rows, width, dist0)
    period = u.shape[1]
    return jnp.tile(u, (1, rows))[:, :rows * (period - 1)].reshape(-1, rows, period - 1)[:, :, :width]


def _band_bias_kernel(u_ref, o_ref):
    n_variants, rows, width = o_ref.shape
    skew = pltpu.roll(jnp.broadcast_to(u_ref[...], (rows, u_ref.shape[-1])), 0, 1, stride=1, stride_axis=0)
    skew = skew[:, :width]
    qi = lax.broadcasted_iota(jnp.int32, (rows, width), 0)
    kn = lax.broadcasted_iota(jnp.int32, (rows, width), 1)
    chunk_gap = qi // CHUNK + (n_variants - 1) * (rows // CHUNK) - kn // CHUNK
    in_band = (chunk_gap >= 0) & (chunk_gap <= LEFT_CHUNKS)
    for v in range(n_variants):
        visible = in_band & (kn // rows >= n_variants - 1 - v)
        o_ref[v] = jnp.where(visible, skew, NEG_INF)


def _band_bias(table):
    width = KV_BLOCKS * Q_BLOCK
    u = _lag_vector(table, Q_BLOCK, width, (KV_BLOCKS - 1) * Q_BLOCK).astype(F32)
    n_heads, period = u.shape
    return pl.pallas_call(
        _band_bias_kernel,
        grid=(n_heads,),
        in_specs=[pl.BlockSpec((None, 1, period), lambda h: (h, 0, 0))],
        out_specs=pl.BlockSpec((KV_BLOCKS, None, Q_BLOCK, width), lambda h: (0, h, 0, 0)),
        out_shape=jax.ShapeDtypeStruct((KV_BLOCKS, n_heads, Q_BLOCK, width), F32),
        compiler_params=_params(("parallel",), 32),
        name="band_bias",
    )(u.reshape(n_heads, 1, period))


def _step_bias(table, s, w):
    q_pos = PAST_LEN + np.arange(s)
    k_pos = np.concatenate([PAST_LEN - w + np.arange(w), q_pos])
    qch, kch = q_pos // CHUNK, k_pos // CHUNK
    mask = (kch[None, :] <= qch[:, None]) & (kch[None, :] >= qch[:, None] - LEFT_CHUNKS)
    bias_c = jnp.where(mask[None, :, :w], _skewed_bias(table, s, w, w), NEG_INF).astype(F32)
    bias_n = jnp.where(mask[None, :, w:], _skewed_bias(table, s, s, 0), NEG_INF).astype(F32)
    return bias_c, bias_n


def _ssd_scan_kernel(xs_ref, b_ref, c_ref, dt_ref, z_ref, h0_ref, a_ref, dskip_ref, gg_ref,
                     yn_ref, hout_ref, state_ref, y_ref, src_t_ref, w_t_ref, *, rows):
    lc = SCAN_CHUNK
    i = pl.program_id(1)

    @pl.when(i == 0)
    def _():
        state_ref[...] = h0_ref[...]

    row_id = lax.broadcasted_iota(jnp.int32, (lc, lc), 0)
    col_id = lax.broadcasted_iota(jnp.int32, (lc, lc), 1)
    causal = col_id <= row_id
    causal_f = causal.astype(F32)
    low_half = lax.broadcasted_iota(jnp.int32, (lc, PAIR), 1) < SSM_HEADDIM
    low_half_row = lax.broadcasted_iota(jnp.int32, (1, PAIR), 1) < SSM_HEADDIM

    def chunk(c, carry):
        rows_c = pl.ds(pl.multiple_of(c * lc, lc), lc)
        dt = dt_ref[rows_c, :]
        a_cs = jnp.dot(causal_f, dt * (a_ref[...] * LOG2E), precision=lax.Precision.HIGHEST,
                       preferred_element_type=F32)
        a_cs_t = a_cs.T
        dt_t = dt.T
        src_t_ref[...] = a_cs_t - jnp.log2(dt_t)
        w_t_ref[...] = dt_t * jnp.exp2(a_cs_t[:, lc - 1:lc] - a_cs_t)
        chunk_decay = jnp.exp2(a_cs[lc - 1:lc, :])
        for g in range(N_GROUPS):
            gcols = slice(g * D_STATE, (g + 1) * D_STATE)
            bg = b_ref[rows_c, gcols]
            cg = c_ref[rows_c, gcols]
            cb = lax.dot_general(cg, bg, NT_DIMS, preferred_element_type=F32)
            bg_t = bg.astype(F32).T
            cg_f = cg.astype(F32)
            for jp in range(PAIRS_PER_GROUP):
                q = g * PAIRS_PER_GROUP + jp
                pcols = slice(q * PAIR, (q + 1) * PAIR)
                xp = xs_ref[rows_c, pcols]
                h_t = state_ref[q]
                lhs_y, lhs_s = [], []
                for h in (2 * q, 2 * q + 1):
                    col = jnp.broadcast_to(a_cs[:, h:h + 1], (lc, lc))
                    row = jnp.broadcast_to(src_t_ref[h:h + 1, :], (lc, lc))
                    within = cb * jnp.exp2(jnp.where(causal, col - row, -jnp.inf))
                    carried = cg_f * jnp.exp2(col)
                    lhs_y.append(jnp.concatenate([within.astype(BF16), carried.astype(BF16)], axis=1))
                    lhs_s.append((bg_t * jnp.broadcast_to(w_t_ref[h:h + 1, :], (D_STATE, lc))).astype(BF16))
                y2 = jnp.dot(jnp.concatenate(lhs_y, axis=0), jnp.concatenate([xp, h_t.astype(BF16)], axis=0),
                             preferred_element_type=F32)
                s2 = jnp.dot(jnp.concatenate(lhs_s, axis=0), xp, preferred_element_type=F32)
                decay = jnp.where(low_half_row,
                                  jnp.broadcast_to(chunk_decay[:, 2 * q:2 * q + 1], (1, PAIR)),
                                  jnp.broadcast_to(chunk_decay[:, 2 * q + 1:2 * q + 2], (1, PAIR)))
                state_ref[q] = h_t * decay + jnp.where(low_half, s2[:D_STATE], s2[D_STATE:])
                y_ref[rows_c, pcols] = (jnp.where(low_half, y2[:lc], y2[lc:])
                                        + dskip_ref[:, pcols] * xp.astype(F32))
        return carry

    lax.fori_loop(0, rows // lc, chunk, 0)

    y = y_ref[...] * _silu(z_ref[...].astype(F32))
    yn_ref[...] = (y * _rms_scale(y) * gg_ref[...]).astype(yn_ref.dtype)

    @pl.when(i == pl.num_programs(1) - 1)
    def _():
        hout_ref[...] = state_ref[...]


def _ssd_scan(xbc, dt, z, h0, a_neg, d_skip, gate_g, *, rows):
    b, l, _ = xbc.shape
    gn = N_GROUPS * D_STATE
    b_block = D_INNER // gn
    return pl.pallas_call(
        functools.partial(_ssd_scan_kernel, rows=rows),
        grid=(b, l // rows),
        in_specs=[
            pl.BlockSpec((None, rows, D_INNER), lambda bb, i: (bb, i, 0)),
            pl.BlockSpec((None, rows, gn), lambda bb, i: (bb, i, b_block)),
            pl.BlockSpec((None, rows, gn), lambda bb, i: (bb, i, b_block + 1)),
            pl.BlockSpec((None, rows, LANES), lambda bb, i: (bb, i, 0)),
            pl.BlockSpec((None, rows, D_INNER), lambda bb, i: (bb, i, 0)),
            pl.BlockSpec((None, N_PAIRS, D_STATE, PAIR), lambda bb, i: (bb, 0, 0, 0)),
            pl.BlockSpec((1, LANES), lambda bb, i: (0, 0)),
            pl.BlockSpec((1, D_INNER), lambda bb, i: (0, 0)),
            pl.BlockSpec((1, D_INNER), lambda bb, i: (0, 0)),
        ],
        out_specs=[
            pl.BlockSpec((None, rows, D_INNER), lambda bb, i: (bb, i, 0)),
            pl.BlockSpec((None, N_PAIRS, D_STATE, PAIR), lambda bb, i: (bb, 0, 0, 0)),
        ],
        out_shape=[
            jax.ShapeDtypeStruct((b, l, D_INNER), BF16),
            jax.ShapeDtypeStruct((b, N_PAIRS, D_STATE, PAIR), F32),
        ],
        scratch_shapes=[
            pltpu.VMEM((N_PAIRS, D_STATE, PAIR), F32),
            pltpu.VMEM((rows, D_INNER), F32),
            pltpu.VMEM((LANES, SCAN_CHUNK), F32),
            pltpu.VMEM((LANES, SCAN_CHUNK), F32),
        ],
        compiler_params=_params(("arbitrary", "arbitrary"), 48),
        name="ssd_scan",
    )(xbc, xbc, xbc, dt, z, h0, a_neg, d_skip, gate_g)


def _to_pair_layout(h):
    b = h.shape[0]
    return h.reshape(b, N_PAIRS, 2, SSM_HEADDIM, D_STATE).transpose(0, 1, 4, 2, 3).reshape(b, N_PAIRS, D_STATE, PAIR)


def _from_pair_layout(h):
    b = h.shape[0]
    return (h.reshape(b, N_PAIRS, D_STATE, 2, SSM_HEADDIM).transpose(0, 1, 3, 4, 2)
            .reshape(b, N_SSM_HEADS, SSM_HEADDIM, D_STATE))


def _ffn_layer(xp, xs, g, w_gu, w_d, layer, half):
    xs, w_g, w_u, w_dn = _ffn_cast(xs, g, w_gu, w_d, layer, half, tf=512)
    xp = _ffn(xp, g, w_g, w_u, w_dn, tm=1024, tf=512, sub=MXU_DIM)
    return xp, xs


def _attn_layer(xp, xs, cache_k, cache_v, g, w_qkv, w_o, layer, q_gain, k_gain, table, *, batch, seq):
    dec_batch, w = cache_k.shape[1], cache_k.shape[2]
    dec_seq = xs.shape[0] // dec_batch
    gains = jnp.stack([q_gain * (HEAD_DIM ** -0.5 * LOG2E), k_gain]).astype(F32)
    m_s = dec_batch * dec_seq
    qk_s, v_s, w_qk, w_v = _qkv(xs, g, w_qkv, w_qkv, layer, 2 * D_MODEL, gains, tm=m_s, out_dtype=F32,
                                emit_bf16=True)
    w_qk, w_v = w_qk[None], w_v[None]
    ks_new = qk_s[:, D_MODEL:].reshape(dec_batch, dec_seq, N_HEADS, HEAD_DIM)
    vs_new = v_s.reshape(dec_batch, dec_seq, N_HEADS, HEAD_DIM)
    bias_c, bias_n = _step_bias(table, dec_seq, w)
    o_s = _step_attn(qk_s.reshape(dec_batch, dec_seq, 2 * D_MODEL), v_s.reshape(dec_batch, dec_seq, D_MODEL),
                     cache_k, cache_v, layer, bias_c, bias_n)
    xs = _proj_res(o_s.reshape(m_s, D_MODEL), w_o, layer, xs, tm=m_s)
    qk, v = _qkv(xp, g, w_qk, w_v, 0, 0, gains, tm=1024, out_dtype=BF16)
    win = min(LEFT_CHUNKS * CHUNK, seq)
    x_tail = xp.reshape(batch, seq, D_MODEL)[:, seq - win:].reshape(batch * win, D_MODEL)
    qk_tail, v_tail = _qkv(x_tail, g, w_qk, w_v, 0, 0, gains, tm=batch * win, out_dtype=F32)
    k_new = qk_tail[:, D_MODEL:].reshape(batch, win, N_HEADS, HEAD_DIM)
    v_new = v_tail.reshape(batch, win, N_HEADS, HEAD_DIM)
    o = _band_attn(qk.reshape(batch, seq, 2 * D_MODEL), v.reshape(batch, seq, D_MODEL), _band_bias(table))
    xp = _proj_res(o.reshape(batch * seq, D_MODEL), w_o, layer, xp, tm=1024)
    return xp, xs, k_new, v_new, ks_new, vs_new


def _ssd_stream(x2d, h0_pairs, conv_init, p, *, batch, seq, conv_batch, tm, rows):
    g, w_in, w_out, layer, w_dt, conv_w, conv_b, dt_bias, a_neg, d_skip, gate_g = p
    row_tile = min(1024, batch * seq)
    z, dt = _ssd_z_dt(x2d, g, w_in, layer, w_dt, dt_bias, tm=row_tile)
    xbc, tails = _ssd_xbc(x2d.reshape(conv_batch, -1, D_MODEL), g, w_in, layer, conv_w, conv_b, conv_init, tm=tm)
    xbc = xbc.reshape(batch, seq, CONV_DIM)
    z = z.reshape(batch, seq, D_INNER)
    dt = dt.reshape(batch, seq, LANES)
    pad = (-seq) % SCAN_CHUNK
    if pad:
        widen = lambda t: jnp.pad(t, ((0, 0), (0, pad), (0, 0)))
        xbc, z, dt = widen(xbc), widen(z), widen(dt)
    yn, h_new = _ssd_scan(xbc, dt, z, h0_pairs, a_neg, d_skip, gate_g, rows=rows)
    yn = yn[:, :seq].reshape(batch * seq, D_INNER)
    x2d = _proj_res(yn, w_out, layer, x2d, tm=row_tile)
    conv_state = tails[:, -1, :, SUBLANES - (CONV_WIDTH - 1):].reshape(batch, CONV_WIDTH - 1, CONV_DIM)
    return x2d, _from_pair_layout(h_new), conv_state


def _ssd_layer(xp, xs, state_ssm, state_conv, g, w_in, w_out, layer, w_dt, conv_w, conv_b, dt_bias, a_log,
               d_skip, gate_g, *, batch, seq):
    dec_batch = state_ssm.shape[0]
    dec_seq = xs.shape[0] // dec_batch
    lane_pad = (0, LANES - N_SSM_HEADS)
    p = (g, w_in, w_out, layer, w_dt, conv_w, conv_b,
         jnp.pad(dt_bias.astype(F32), lane_pad).reshape(1, LANES),
         jnp.pad(-jnp.exp(a_log.astype(F32)), lane_pad).reshape(1, LANES),
         jnp.repeat(d_skip, SSM_HEADDIM).reshape(1, D_INNER),
         gate_g.reshape(1, D_INNER))
    conv_init = jnp.pad(state_conv, ((0, 0), (SUBLANES - (CONV_WIDTH - 1), 0), (0, 0)))[None]
    xs, hs, cs = _ssd_stream(xs, _to_pair_layout(state_ssm), conv_init, p, batch=dec_batch, seq=dec_seq,
                             conv_batch=1, tm=dec_batch * dec_seq, rows=SCAN_CHUNK)
    zero_state = jnp.zeros((batch, N_PAIRS, D_STATE, PAIR), F32)
    zero_conv = jnp.zeros((batch, 1, SUBLANES, CONV_DIM), F32)
    xp, hp, cp = _ssd_stream(xp, zero_state, zero_conv, p, batch=batch, seq=seq, conv_batch=batch, tm=1024,
                             rows=2 * SCAN_CHUNK)
    return xp, xs, hp, cp, hs, cs


def kernel(x_prompt, x_sample, cache_k, cache_v, state_ssm, state_conv, ffn_norm, ffn_w_gate_up, ffn_w_down, attn_norm, attn_w_qkv, attn_q_norm, attn_k_norm, attn_rel_bias, attn_w_o, ssd_norm, ssd_w_in, ssd_conv_w, ssd_conv_b, ssd_dt_bias, ssd_a_log, ssd_d_skip, ssd_gate_norm, ssd_w_out):
    batch, seq, _ = x_prompt.shape
    dec_batch, dec_seq, _ = x_sample.shape
    depth = ffn_norm.shape[0]
    xp = x_prompt.reshape(batch * seq, D_MODEL)
    xs = x_sample.reshape(dec_batch * dec_seq, D_MODEL)

    w_o = attn_w_o.astype(BF16)
    w_in = ssd_w_in.astype(BF16)
    w_out = ssd_w_out.astype(BF16)
    w_dt = jnp.pad(w_in[:, :, D_INNER + CONV_DIM:], ((0, 0), (0, 0), (0, LANES - N_SSM_HEADS)))

    kp, vp, hp, cp, ks, vs, hs, cs = [], [], [], [], [], [], [], []
    for i in range(depth):
        j = i // 2
        xp, xs = _ffn_layer(xp, xs, ffn_norm[i, 0], ffn_w_gate_up, ffn_w_down, i, 0)
        if i % 2 == 0:
            xp, xs, k_new, v_new, ks_new, vs_new = _attn_layer(
                xp, xs, cache_k, cache_v, attn_norm[j], attn_w_qkv, w_o, j, attn_q_norm[j], attn_k_norm[j],
                attn_rel_bias[j], batch=batch, seq=seq)
            kp.append(k_new); vp.append(v_new); ks.append(ks_new); vs.append(vs_new)
        else:
            xp, xs, h_p, c_p, h_s, c_s = _ssd_layer(
                xp, xs, state_ssm[j], state_conv[j], ssd_norm[j], w_in, w_out, j,
                w_dt, ssd_conv_w[j], ssd_conv_b[j], ssd_dt_bias[j], ssd_a_log[j],
                ssd_d_skip[j], ssd_gate_norm[j], batch=batch, seq=seq)
            hp.append(h_p); cp.append(c_p); hs.append(h_s); cs.append(c_s)
        xp, xs = _ffn_layer(xp, xs, ffn_norm[i, 1], ffn_w_gate_up, ffn_w_down, i, 1)

    return (xp.reshape(batch, seq, D_MODEL), xs.reshape(dec_batch, dec_seq, D_MODEL),
            jnp.stack(kp), jnp.stack(vp), jnp.stack(hp), jnp.stack(cp),
            jnp.stack(ks), jnp.stack(vs), jnp.stack(hs), jnp.stack(cs))
```

```python
import functools
import math

import jax
import jax.numpy as jnp
import numpy as np
from jax import lax
from jax.experimental import pallas as pl
from jax.experimental.pallas import tpu as pltpu

F32 = jnp.float32
BF16 = jnp.bfloat16

D_MODEL = 2048
CHUNK = 64
LEFT_CHUNKS = 8
HEAD_DIM = 128
N_HEADS = D_MODEL // HEAD_DIM
REL_CLIP = 128
PAST_LEN = 1024
D_INNER = 2 * D_MODEL
SSM_HEADDIM = 64
N_SSM_HEADS = D_INNER // SSM_HEADDIM
N_GROUPS = 8
D_STATE = 128
CONV_WIDTH = 4
CONV_DIM = D_INNER + 2 * N_GROUPS * D_STATE
EPS = 1e-6
NEG_INF = -1e30
LOG2E = math.log2(math.e)

LANES = 128
SUBLANES = 8
MXU_DIM = 256
MIB = 1024 * 1024

Q_BLOCK = 4 * CHUNK
KV_BLOCKS = LEFT_CHUNKS * CHUNK // Q_BLOCK + 1
SCAN_CHUNK = 128
PAIR = 2 * SSM_HEADDIM
N_PAIRS = N_SSM_HEADS // 2
PAIRS_PER_GROUP = N_PAIRS // N_GROUPS
COL_TILE = 1024
assert SCAN_CHUNK == D_STATE == PAIR == LANES

NT_DIMS = (((1,), (1,)), ((), ()))


def _params(semantics, vmem_mib):
    return pltpu.CompilerParams(dimension_semantics=semantics, vmem_limit_bytes=vmem_mib * MIB)


def _rms_scale(x):
    return lax.rsqrt(jnp.mean(x * x, axis=-1, keepdims=True) + EPS)


def _silu(x):
    return x * jax.nn.sigmoid(x)


def _with_norm_on_first(x_ref, g_ref, xn_ref, first, work):
    @pl.when(first)
    def _():
        x = x_ref[...]
        xn_ref[...] = (x * _rms_scale(x) * g_ref[...]).astype(BF16)
        work()

    @pl.when(jnp.logical_not(first))
    def _():
        work()


def _ffn_cast_kernel(x_ref, g_ref, wg_ref, wu_ref, wd_ref, o_ref, wgb_ref, wub_ref, wdb_ref, xn_ref):
    @pl.when(pl.program_id(0) == 0)
    def _():
        x = x_ref[...]
        xn_ref[...] = (x * _rms_scale(x) * g_ref[...]).astype(BF16)
        o_ref[...] = x

    wg = wg_ref[...].astype(BF16)
    wu = wu_ref[...].astype(BF16)
    wd = wd_ref[...].astype(BF16)
    wgb_ref[...] = wg
    wub_ref[...] = wu
    wdb_ref[...] = wd
    xn = xn_ref[...]
    gate = jnp.dot(xn, wg, preferred_element_type=F32)
    up = jnp.dot(xn, wu, preferred_element_type=F32)
    act = (_silu(gate) * up).astype(BF16)
    o_ref[...] += 0.5 * jnp.dot(act, wd, preferred_element_type=F32)


def _ffn_cast(x, g, w_gu, w_d, layer, half, *, tf):
    m, d = x.shape
    d_ff = w_d.shape[2]
    nf = d_ff // tf
    return pl.pallas_call(
        _ffn_cast_kernel,
        grid=(nf,),
        in_specs=[
            pl.BlockSpec((m, d), lambda f: (0, 0)),
            pl.BlockSpec((1, d), lambda f: (0, 0)),
            pl.BlockSpec((None, None, d, tf), lambda f: (layer, half, 0, f)),
            pl.BlockSpec((None, None, d, tf), lambda f: (layer, half, 0, f + nf)),
            pl.BlockSpec((None, None, tf, d), lambda f: (layer, half, f, 0)),
        ],
        out_specs=[
            pl.BlockSpec((m, d), lambda f: (0, 0)),
            pl.BlockSpec((d, tf), lambda f: (0, f)),
            pl.BlockSpec((d, tf), lambda f: (0, f)),
            pl.BlockSpec((tf, d), lambda f: (f, 0)),
        ],
        out_shape=[
            jax.ShapeDtypeStruct((m, d), F32),
            jax.ShapeDtypeStruct((d, d_ff), BF16),
            jax.ShapeDtypeStruct((d, d_ff), BF16),
            jax.ShapeDtypeStruct((d_ff, d), BF16),
        ],
        scratch_shapes=[pltpu.VMEM((m, d), BF16)],
        compiler_params=_params(("arbitrary",), 56),
        name="ffn_cast",
    )(x, g.reshape(1, d), w_gu, w_gu, w_d)


def _ffn_kernel(x_ref, g_ref, wg_ref, wu_ref, wd_ref, o_ref, xn_ref, *, sub):
    def half_down():
        xn = xn_ref[...]
        down = None
        for c in range(wg_ref.shape[1] // sub):
            cs = slice(c * sub, (c + 1) * sub)
            gate = jnp.dot(xn, wg_ref[:, cs], preferred_element_type=F32)
            up = jnp.dot(xn, wu_ref[:, cs], preferred_element_type=F32)
            act = (_silu(gate) * up).astype(BF16)
            part = jnp.dot(act, wd_ref[cs, :], preferred_element_type=F32)
            down = part if down is None else down + part
        return 0.5 * down

    first = pl.program_id(1) == 0

    @pl.when(first)
    def _():
        x = x_ref[...]
        xn_ref[...] = (x * _rms_scale(x) * g_ref[...]).astype(BF16)
        o_ref[...] = x_ref[...] + half_down()

    @pl.when(jnp.logical_not(first))
    def _():
        o_ref[...] += half_down()


def _ffn(x, g, w_g, w_u, w_d, *, tm, tf, sub):
    m, d = x.shape
    d_ff = w_d.shape[0]
    return pl.pallas_call(
        functools.partial(_ffn_kernel, sub=sub),
        grid=(m // tm, d_ff // tf),
        in_specs=[
            pl.BlockSpec((tm, d), lambda i, f: (i, 0)),
            pl.BlockSpec((1, d), lambda i, f: (0, 0)),
            pl.BlockSpec((d, tf), lambda i, f: (0, f)),
            pl.BlockSpec((d, tf), lambda i, f: (0, f)),
            pl.BlockSpec((tf, d), lambda i, f: (f, 0)),
        ],
        out_specs=pl.BlockSpec((tm, d), lambda i, f: (i, 0)),
        out_shape=jax.ShapeDtypeStruct((m, d), F32),
        scratch_shapes=[pltpu.VMEM((tm, d), BF16)],
        compiler_params=_params(("parallel", "arbitrary"), 58),
        name="ffn",
    )(x, g.reshape(1, d), w_g, w_u, w_d)


def _qkv_kernel(x_ref, g_ref, wqk_ref, wv_ref, gain_ref, qk_ref, v_ref, *rest, sub, emit_bf16):
    xn_ref = rest[-1]
    n = pl.program_id(1)

    def work():
        if emit_bf16:
            wqk, wv = rest[0], rest[1]
            wqk[...] = wqk_ref[...].astype(BF16)
            wv[...] = wv_ref[...].astype(BF16)
        else:
            wqk, wv = wqk_ref, wv_ref
        xn = xn_ref[...]
        gain = gain_ref[pl.ds(n // (pl.num_programs(1) // 2), 1), :]
        for c in range(wqk_ref.shape[1] // sub):
            acc = jnp.dot(xn, wqk[:, c * sub:(c + 1) * sub], preferred_element_type=F32)
            for h in range(sub // HEAD_DIM):
                a = acc[:, h * HEAD_DIM:(h + 1) * HEAD_DIM]
                cols = slice(c * sub + h * HEAD_DIM, c * sub + (h + 1) * HEAD_DIM)
                qk_ref[:, cols] = (a * _rms_scale(a) * gain).astype(qk_ref.dtype)
        v_ref[...] = jnp.dot(xn, wv[...], preferred_element_type=F32).astype(v_ref.dtype)

    _with_norm_on_first(x_ref, g_ref, xn_ref, n == 0, work)


def _qkv(x, g, w_qk, w_v, layer, v_col0, gains, *, tm, out_dtype, emit_bf16=False):
    m, d = x.shape
    n_tiles = 2 * D_MODEL // COL_TILE
    tv = D_MODEL // n_tiles
    v0 = v_col0 // tv
    assert not emit_bf16 or m == tm
    out_specs = [
        pl.BlockSpec((tm, COL_TILE), lambda i, n: (i, n)),
        pl.BlockSpec((tm, tv), lambda i, n: (i, n)),
    ]
    out_shape = [
        jax.ShapeDtypeStruct((m, 2 * D_MODEL), out_dtype),
        jax.ShapeDtypeStruct((m, D_MODEL), out_dtype),
    ]
    if emit_bf16:
        out_specs += [pl.BlockSpec((d, COL_TILE), lambda i, n: (0, n)), pl.BlockSpec((d, tv), lambda i, n: (0, n))]
        out_shape += [jax.ShapeDtypeStruct((d, 2 * D_MODEL), BF16), jax.ShapeDtypeStruct((d, D_MODEL), BF16)]
    return pl.pallas_call(
        functools.partial(_qkv_kernel, sub=MXU_DIM, emit_bf16=emit_bf16),
        grid=(m // tm, n_tiles),
        in_specs=[
            pl.BlockSpec((tm, d), lambda i, n: (i, 0)),
            pl.BlockSpec((1, d), lambda i, n: (0, 0)),
            pl.BlockSpec((None, d, COL_TILE), lambda i, n: (layer, 0, n)),
            pl.BlockSpec((None, d, tv), lambda i, n: (layer, 0, v0 + n)),
            pl.BlockSpec((2, HEAD_DIM), lambda i, n: (0, 0)),
        ],
        out_specs=out_specs,
        out_shape=out_shape,
        scratch_shapes=[pltpu.VMEM((tm, d), BF16)],
        compiler_params=_params(("parallel", "arbitrary"), 48),
        name="qkv",
    )(x, g.reshape(1, d), w_qk, w_v, gains)


def _z_dt_kernel(x_ref, g_ref, w_ref, wdt_ref, b_ref, z_ref, dt_ref, xn_ref):
    def project():
        z_ref[...] = jnp.dot(xn_ref[...], w_ref[...], preferred_element_type=F32).astype(z_ref.dtype)

    def project_with_dt():
        raw = jnp.dot(xn_ref[...], wdt_ref[...], preferred_element_type=F32) + b_ref[...]
        dt_ref[...] = jnp.maximum(raw, 0.0) + jnp.log1p(jnp.exp(-jnp.abs(raw)))
        project()

    first = pl.program_id(1) == 0

    @pl.when(first)
    def _():
        x = x_ref[...]
        xn_ref[...] = (x * _rms_scale(x) * g_ref[...]).astype(BF16)
        project_with_dt()

    @pl.when(jnp.logical_not(first))
    def _():
        project()


def _ssd_z_dt(x, g, w_in, layer, w_dt, dt_bias, *, tm):
    m, d = x.shape
    tn = 2 * COL_TILE
    return pl.pallas_call(
        _z_dt_kernel,
        grid=(m // tm, D_INNER // tn),
        in_specs=[
            pl.BlockSpec((tm, d), lambda i, j: (i, 0)),
            pl.BlockSpec((1, d), lambda i, j: (0, 0)),
            pl.BlockSpec((None, d, tn), lambda i, j: (layer, 0, j)),
            pl.BlockSpec((None, d, LANES), lambda i, j: (layer, 0, 0)),
            pl.BlockSpec((1, LANES), lambda i, j: (0, 0)),
        ],
        out_specs=[
            pl.BlockSpec((tm, tn), lambda i, j: (i, j)),
            pl.BlockSpec((tm, LANES), lambda i, j: (i, 0)),
        ],
        out_shape=[
            jax.ShapeDtypeStruct((m, D_INNER), BF16),
            jax.ShapeDtypeStruct((m, LANES), F32),
        ],
        scratch_shapes=[pltpu.VMEM((tm, d), BF16)],
        compiler_params=_params(("parallel", "arbitrary"), 56),
        name="ssd_z_dt",
    )(x, g.reshape(1, d), w_in, w_dt, dt_bias)


def _xbc_kernel(x_ref, g_ref, w_ref, cw_ref, cb_ref, init_ref, o_ref, st_ref, xn_ref, carry_ref, *, tm, sub,
                row_block, streams):
    i = pl.program_id(1)
    n = pl.program_id(2)
    chained = streams == 1

    if chained:
        @pl.when(i == 0)
        def _():
            carry_ref[n] = init_ref[0]

    rb = row_block
    sublane_id = lax.broadcasted_iota(jnp.int32, (1, SUBLANES, sub), 1)

    def work():
        for c in range(w_ref.shape[1] // sub):
            cs = slice(c * sub, (c + 1) * sub)
            taps = [cw_ref[t:t + 1, cs] for t in range(CONV_WIDTH)]
            carry = carry_ref[n, :, cs] if chained else None
            for r in range(tm // rb):
                rows = slice(r * rb, (r + 1) * rb)
                if not chained:
                    carry = init_ref[r, :, cs]
                raw = jnp.dot(xn_ref[rows, :], w_ref[:, cs], preferred_element_type=F32)
                tiles = jnp.concatenate([carry, raw], axis=0).reshape(rb // SUBLANES + 1, SUBLANES, sub)
                conv = cb_ref[:, cs] + raw * taps[CONV_WIDTH - 1]
                conv = conv.reshape(rb // SUBLANES, SUBLANES, sub)
                for t in range(CONV_WIDTH - 1):
                    lag = CONV_WIDTH - 1 - t
                    rot = pltpu.roll(tiles, lag, 1)
                    conv = conv + jnp.where(sublane_id < lag, rot[:-1], rot[1:]) * taps[t]
                o_ref[rows, cs] = _silu(conv).reshape(rb, sub).astype(o_ref.dtype)
                carry = raw[rb - SUBLANES:rb]
                if not chained:
                    st_ref[r, :, cs] = carry
            if chained:
                carry_ref[n, :, cs] = carry
                st_ref[0, :, cs] = carry

    _with_norm_on_first(x_ref, g_ref, xn_ref, n == 0, work)


def _ssd_xbc(x, g, w_in, layer, conv_w, conv_b, init, *, tm):
    b, l, d = x.shape
    streams = init.shape[1]
    tn = 2 * COL_TILE
    n_tiles = CONV_DIM // tn
    tile0 = D_INNER // tn
    assert streams == 1 or l == tm
    rb = min(tm, 512) if streams == 1 else tm // streams
    out_specs = [
        pl.BlockSpec((None, tm, tn), lambda bb, i, n: (bb, i, n)),
        pl.BlockSpec((None, None, streams, SUBLANES, tn), lambda bb, i, n: (bb, i, 0, 0, n)),
    ]
    out_shape = [
        jax.ShapeDtypeStruct((b, l, CONV_DIM), BF16),
        jax.ShapeDtypeStruct((b, l // tm, streams, SUBLANES, CONV_DIM), F32),
    ]
    return pl.pallas_call(
        functools.partial(_xbc_kernel, tm=tm, sub=MXU_DIM, row_block=rb, streams=streams),
        grid=(b, l // tm, n_tiles),
        in_specs=[
            pl.BlockSpec((None, tm, d), lambda bb, i, n: (bb, i, 0)),
            pl.BlockSpec((1, d), lambda bb, i, n: (0, 0)),
            pl.BlockSpec((None, d, tn), lambda bb, i, n: (layer, 0, tile0 + n)),
            pl.BlockSpec((CONV_WIDTH, tn), lambda bb, i, n: (0, n)),
            pl.BlockSpec((1, tn), lambda bb, i, n: (0, n)),
            pl.BlockSpec((None, streams, SUBLANES, tn), lambda bb, i, n: (bb, 0, 0, n)),
        ],
        out_specs=out_specs,
        out_shape=out_shape,
        scratch_shapes=[
            pltpu.VMEM((tm, d), BF16),
            pltpu.VMEM((n_tiles, SUBLANES, tn), F32),
        ],
        compiler_params=_params(("arbitrary", "arbitrary", "arbitrary"), 56),
        name="ssd_xbc",
    )(x, g.reshape(1, d), w_in, conv_w, conv_b.reshape(1, CONV_DIM), init)


def _proj_res_kernel(a_ref, w_ref, x_ref, o_ref):
    o_ref[...] = x_ref[...] + jnp.dot(a_ref[...], w_ref[...], preferred_element_type=F32)


def _proj_res(a, w, layer, x, *, tm):
    m, k = a.shape
    n = w.shape[2]
    return pl.pallas_call(
        _proj_res_kernel,
        grid=(n // COL_TILE, m // tm),
        in_specs=[
            pl.BlockSpec((tm, k), lambda j, i: (i, 0)),
            pl.BlockSpec((None, k, COL_TILE), lambda j, i: (layer, 0, j)),
            pl.BlockSpec((tm, COL_TILE), lambda j, i: (i, j)),
        ],
        out_specs=pl.BlockSpec((tm, COL_TILE), lambda j, i: (i, j)),
        out_shape=jax.ShapeDtypeStruct((m, n), F32),
        compiler_params=_params(("parallel", "parallel"), 56),
        name="proj_res",
    )(a, w, x)


def _band_attn_kernel(q_ref, k0_ref, k1_ref, k2_ref, v0_ref, v1_ref, v2_ref, bias_ref, o_ref):
    k_refs = (k0_ref, k1_ref, k2_ref)
    v_refs = (v0_ref, v1_ref, v2_ref)
    for h in range(N_HEADS):
        cols = slice(h * HEAD_DIM, (h + 1) * HEAD_DIM)
        qh = q_ref[:, cols]
        scores = [lax.dot_general(qh, k_refs[t][:, cols], NT_DIMS, preferred_element_type=F32)
                  + bias_ref[h, :, t * Q_BLOCK:(t + 1) * Q_BLOCK] for t in range(KV_BLOCKS)]
        row_max = jnp.max(functools.reduce(jnp.maximum, scores), axis=-1, keepdims=True)
        probs = [jnp.exp2(s - row_max) for s in scores]
        denom = jnp.sum(functools.reduce(jnp.add, probs), axis=-1, keepdims=True)
        out = functools.reduce(jnp.add, [
            jnp.dot(p.astype(BF16), v_refs[t][:, cols], preferred_element_type=F32)
            for t, p in enumerate(probs)])
        o_ref[:, cols] = (out / denom).astype(o_ref.dtype)


def _band_attn(qk, v, bias):
    b, l, _ = v.shape

    def kv_spec(col, t):
        return pl.BlockSpec((None, Q_BLOCK, D_MODEL),
                            lambda bb, j: (bb, jnp.maximum(j - (KV_BLOCKS - 1 - t), 0), col))

    return pl.pallas_call(
        _band_attn_kernel,
        grid=(b, l // Q_BLOCK),
        in_specs=[pl.BlockSpec((None, Q_BLOCK, D_MODEL), lambda bb, j: (bb, j, 0))]
        + [kv_spec(1, t) for t in range(KV_BLOCKS)]
        + [kv_spec(0, t) for t in range(KV_BLOCKS)]
        + [pl.BlockSpec((None,) + bias.shape[1:], lambda bb, j: (jnp.minimum(j, KV_BLOCKS - 1), 0, 0, 0))],
        out_specs=pl.BlockSpec((None, Q_BLOCK, D_MODEL), lambda bb, j: (bb, j, 0)),
        out_shape=jax.ShapeDtypeStruct((b, l, D_MODEL), BF16),
        compiler_params=_params(("parallel", "parallel"), 56),
        name="band_attn",
    )(qk, qk, qk, qk, v, v, v, bias)


def _step_attn_kernel(qk_ref, v_ref, kc_ref, vc_ref, bias_c_ref, bias_n_ref, o_ref):
    for h in range(N_HEADS):
        cols = slice(h * HEAD_DIM, (h + 1) * HEAD_DIM)
        qh = qk_ref[:, cols].astype(BF16)
        kn = qk_ref[:, D_MODEL + h * HEAD_DIM:D_MODEL + (h + 1) * HEAD_DIM].astype(BF16)
        vn = v_ref[:, cols].astype(BF16)
        kc = kc_ref[:, h, :].astype(BF16)
        vc = vc_ref[:, h, :].astype(BF16)
        s_c = lax.dot_general(qh, kc, NT_DIMS, preferred_element_type=F32) + bias_c_ref[h]
        s_n = lax.dot_general(qh, kn, NT_DIMS, preferred_element_type=F32) + bias_n_ref[h]
        row_max = jnp.maximum(jnp.max(s_c, axis=-1, keepdims=True), jnp.max(s_n, axis=-1, keepdims=True))
        p_c = jnp.exp2(s_c - row_max)
        p_n = jnp.exp2(s_n - row_max)
        denom = jnp.sum(p_c, axis=-1, keepdims=True) + jnp.sum(p_n, axis=-1, keepdims=True)
        out = (jnp.dot(p_c.astype(BF16), vc, preferred_element_type=F32)
               + jnp.dot(p_n.astype(BF16), vn, preferred_element_type=F32))
        o_ref[:, cols] = (out / denom).astype(o_ref.dtype)


def _step_attn(qk, v, k_cache, v_cache, layer, bias_c, bias_n):
    b, s, _ = v.shape
    w = k_cache.shape[2]
    cache_spec = pl.BlockSpec((None, None, w, N_HEADS, HEAD_DIM), lambda i: (layer, i, 0, 0, 0))
    return pl.pallas_call(
        _step_attn_kernel,
        grid=(b,),
        in_specs=[
            pl.BlockSpec((None, s, 2 * D_MODEL), lambda i: (i, 0, 0)),
            pl.BlockSpec((None, s, D_MODEL), lambda i: (i, 0, 0)),
            cache_spec,
            cache_spec,
            pl.BlockSpec(bias_c.shape, lambda i: (0, 0, 0)),
            pl.BlockSpec(bias_n.shape, lambda i: (0, 0, 0)),
        ],
        out_specs=pl.BlockSpec((None, s, D_MODEL), lambda i: (i, 0, 0)),
        out_shape=jax.ShapeDtypeStruct((b, s, D_MODEL), BF16),
        compiler_params=_params(("parallel",), 48),
        name="step_attn",
    )(qk, v, k_cache, v_cache, bias_c, bias_n)


def _lag_vector(table, rows, width, dist0):
    period = width + rows
    m = np.arange(period)
    lag = np.where(m < width, m, m - period)
    idx = np.clip(dist0 - lag, -REL_CLIP, REL_CLIP) + REL_CLIP
    return table[:, idx] * LOG2E


def _skewed_bias(table, rows, width, dist0):
    u = _lag_vector(table, rows, width, dist0)
    period = u.shape[1]
    return jnp.tile(u, (1, rows))[:, :rows * (period - 1)].reshape(-1, rows, period - 1)[:, :, :width]


def _band_bias_kernel(u_ref, o_ref):
    n_variants, rows, width = o_ref.shape
    skew = pltpu.roll(jnp.broadcast_to(u_ref[...], (rows, u_ref.shape[-1])), 0, 1, stride=1, stride_axis=0)
    skew = skew[:, :width]
    qi = lax.broadcasted_iota(jnp.int32, (rows, width), 0)
    kn = lax.broadcasted_iota(jnp.int32, (rows, width), 1)
    chunk_gap = qi // CHUNK + (n_variants - 1) * (rows // CHUNK) - kn // CHUNK
    in_band = (chunk_gap >= 0) & (chunk_gap <= LEFT_CHUNKS)
    for v in range(n_variants):
        visible = in_band & (kn // rows >= n_variants - 1 - v)
        o_ref[v] = jnp.where(visible, skew, NEG_INF)


def _band_bias(table):
    width = KV_BLOCKS * Q_BLOCK
    u = _lag_vector(table, Q_BLOCK, width, (KV_BLOCKS - 1) * Q_BLOCK).astype(F32)
    n_heads, period = u.shape
    return pl.pallas_call(
        _band_bias_kernel,
        grid=(n_heads,),
        in_specs=[pl.BlockSpec((None, 1, period), lambda h: (h, 0, 0))],
        out_specs=pl.BlockSpec((KV_BLOCKS, None, Q_BLOCK, width), lambda h: (0, h, 0, 0)),
        out_shape=jax.ShapeDtypeStruct((KV_BLOCKS, n_heads, Q_BLOCK, width), F32),
        compiler_params=_params(("parallel",), 32),
        name="band_bias",
    )(u.reshape(n_heads, 1, period))


def _step_bias(table, s, w):
    q_pos = PAST_LEN + np.arange(s)
    k_pos = np.concatenate([PAST_LEN - w + np.arange(w), q_pos])
    qch, kch = q_pos // CHUNK, k_pos // CHUNK
    mask = (kch[None, :] <= qch[:, None]) & (kch[None, :] >= qch[:, None] - LEFT_CHUNKS)
    bias_c = jnp.where(mask[None, :, :w], _skewed_bias(table, s, w, w), NEG_INF).astype(F32)
    bias_n = jnp.where(mask[None, :, w:], _skewed_bias(table, s, s, 0), NEG_INF).astype(F32)
    return bias_c, bias_n


def _ssd_scan_kernel(xs_ref, b_ref, c_ref, dt_ref, z_ref, h0_ref, a_ref, dskip_ref, gg_ref,
                     yn_ref, hout_ref, state_ref, y_ref, src_t_ref, w_t_ref, *, rows):
    lc = SCAN_CHUNK
    i = pl.program_id(1)

    @pl.when(i == 0)
    def _():
        state_ref[...] = h0_ref[...]

    row_id = lax.broadcasted_iota(jnp.int32, (lc, lc), 0)
    col_id = lax.broadcasted_iota(jnp.int32, (lc, lc), 1)
    causal = col_id <= row_id
    causal_f = causal.astype(F32)
    low_half = lax.broadcasted_iota(jnp.int32, (lc, PAIR), 1) < SSM_HEADDIM
    low_half_row = lax.broadcasted_iota(jnp.int32, (1, PAIR), 1) < SSM_HEADDIM

    def chunk(c, carry):
        rows_c = pl.ds(pl.multiple_of(c * lc, lc), lc)
        dt = dt_ref[rows_c, :]
        a_cs = jnp.dot(causal_f, dt * (a_ref[...] * LOG2E), precision=lax.Precision.HIGHEST,
                       preferred_element_type=F32)
        a_cs_t = a_cs.T
        dt_t = dt.T
        src_t_ref[...] = a_cs_t - jnp.log2(dt_t)
        w_t_ref[...] = dt_t * jnp.exp2(a_cs_t[:, lc - 1:lc] - a_cs_t)
        chunk_decay = jnp.exp2(a_cs[lc - 1:lc, :])
        for g in range(N_GROUPS):
            gcols = slice(g * D_STATE, (g + 1) * D_STATE)
            bg = b_ref[rows_c, gcols]
            cg = c_ref[rows_c, gcols]
            cb = lax.dot_general(cg, bg, NT_DIMS, preferred_element_type=F32)
            bg_t = bg.astype(F32).T
            cg_f = cg.astype(F32)
            for jp in range(PAIRS_PER_GROUP):
                q = g * PAIRS_PER_GROUP + jp
                pcols = slice(q * PAIR, (q + 1) * PAIR)
                xp = xs_ref[rows_c, pcols]
                h_t = state_ref[q]
                lhs_y, lhs_s = [], []
                for h in (2 * q, 2 * q + 1):
                    col = jnp.broadcast_to(a_cs[:, h:h + 1], (lc, lc))
                    row = jnp.broadcast_to(src_t_ref[h:h + 1, :], (lc, lc))
                    within = cb * jnp.exp2(jnp.where(causal, col - row, -jnp.inf))
                    carried = cg_f * jnp.exp2(col)
                    lhs_y.append(jnp.concatenate([within.astype(BF16), carried.astype(BF16)], axis=1))
                    lhs_s.append((bg_t * jnp.broadcast_to(w_t_ref[h:h + 1, :], (D_STATE, lc))).astype(BF16))
                y2 = jnp.dot(jnp.concatenate(lhs_y, axis=0), jnp.concatenate([xp, h_t.astype(BF16)], axis=0),
                             preferred_element_type=F32)
                s2 = jnp.dot(jnp.concatenate(lhs_s, axis=0), xp, preferred_element_type=F32)
                decay = jnp.where(low_half_row,
                                  jnp.broadcast_to(chunk_decay[:, 2 * q:2 * q + 1], (1, PAIR)),
                                  jnp.broadcast_to(chunk_decay[:, 2 * q + 1:2 * q + 2], (1, PAIR)))
                state_ref[q] = h_t * decay + jnp.where(low_half, s2[:D_STATE], s2[D_STATE:])
                y_ref[rows_c, pcols] = (jnp.where(low_half, y2[:lc], y2[lc:])
                                        + dskip_ref[:, pcols] * xp.astype(F32))
        return carry

    lax.fori_loop(0, rows // lc, chunk, 0)

    y = y_ref[...] * _silu(z_ref[...].astype(F32))
    yn_ref[...] = (y * _rms_scale(y) * gg_ref[...]).astype(yn_ref.dtype)

    @pl.when(i == pl.num_programs(1) - 1)
    def _():
        hout_ref[...] = state_ref[...]


def _ssd_scan(xbc, dt, z, h0, a_neg, d_skip, gate_g, *, rows):
    b, l, _ = xbc.shape
    gn = N_GROUPS * D_STATE
    b_block = D_INNER // gn
    return pl.pallas_call(
        functools.partial(_ssd_scan_kernel, rows=rows),
        grid=(b, l // rows),
        in_specs=[
            pl.BlockSpec((None, rows, D_INNER), lambda bb, i: (bb, i, 0)),
            pl.BlockSpec((None, rows, gn), lambda bb, i: (bb, i, b_block)),
            pl.BlockSpec((None, rows, gn), lambda bb, i: (bb, i, b_block + 1)),
            pl.BlockSpec((None, rows, LANES), lambda bb, i: (bb, i, 0)),
            pl.BlockSpec((None, rows, D_INNER), lambda bb, i: (bb, i, 0)),
            pl.BlockSpec((None, N_PAIRS, D_STATE, PAIR), lambda bb, i: (bb, 0, 0, 0)),
            pl.BlockSpec((1, LANES), lambda bb, i: (0, 0)),
            pl.BlockSpec((1, D_INNER), lambda bb, i: (0, 0)),
            pl.BlockSpec((1, D_INNER), lambda bb, i: (0, 0)),
        ],
        out_specs=[
            pl.BlockSpec((None, rows, D_INNER), lambda bb, i: (bb, i, 0)),
            pl.BlockSpec((None, N_PAIRS, D_STATE, PAIR), lambda bb, i: (bb, 0, 0, 0)),
        ],
        out_shape=[
            jax.ShapeDtypeStruct((b, l, D_INNER), BF16),
            jax.ShapeDtypeStruct((b, N_PAIRS, D_STATE, PAIR), F32),
        ],
        scratch_shapes=[
            pltpu.VMEM((N_PAIRS, D_STATE, PAIR), F32),
            pltpu.VMEM((rows, D_INNER), F32),
            pltpu.VMEM((LANES, SCAN_CHUNK), F32),
            pltpu.VMEM((LANES, SCAN_CHUNK), F32),
        ],
        compiler_params=_params(("arbitrary", "arbitrary"), 48),
        name="ssd_scan",
    )(xbc, xbc, xbc, dt, z, h0, a_neg, d_skip, gate_g)


def _to_pair_layout(h):
    b = h.shape[0]
    return h.reshape(b, N_PAIRS, 2, SSM_HEADDIM, D_STATE).transpose(0, 1, 4, 2, 3).reshape(b, N_PAIRS, D_STATE, PAIR)


def _from_pair_layout(h):
    b = h.shape[0]
    return (h.reshape(b, N_PAIRS, D_STATE, 2, SSM_HEADDIM).transpose(0, 1, 3, 4, 2)
            .reshape(b, N_SSM_HEADS, SSM_HEADDIM, D_STATE))


def _ffn_layer(xp, xs, g, w_gu, w_d, layer, half):
    xs, w_g, w_u, w_dn = _ffn_cast(xs, g, w_gu, w_d, layer, half, tf=512)
    xp = _ffn(xp, g, w_g, w_u, w_dn, tm=1024, tf=512, sub=MXU_DIM)
    return xp, xs


def _attn_layer(xp, xs, cache_k, cache_v, g, w_qkv, w_o, layer, q_gain, k_gain, table, *, batch, seq):
    dec_batch, w = cache_k.shape[1], cache_k.shape[2]
    dec_seq = xs.shape[0] // dec_batch
    gains = jnp.stack([q_gain * (HEAD_DIM ** -0.5 * LOG2E), k_gain]).astype(F32)
    m_s = dec_batch * dec_seq
    qk_s, v_s, w_qk, w_v = _qkv(xs, g, w_qkv, w_qkv, layer, 2 * D_MODEL, gains, tm=m_s, out_dtype=F32,
                                emit_bf16=True)
    w_qk, w_v = w_qk[None], w_v[None]
    ks_new = qk_s[:, D_MODEL:].reshape(dec_batch, dec_seq, N_HEADS, HEAD_DIM)
    vs_new = v_s.reshape(dec_batch, dec_seq, N_HEADS, HEAD_DIM)
    bias_c, bias_n = _step_bias(table, dec_seq, w)
    o_s = _step_attn(qk_s.reshape(dec_batch, dec_seq, 2 * D_MODEL), v_s.reshape(dec_batch, dec_seq, D_MODEL),
                     cache_k, cache_v, layer, bias_c, bias_n)
    xs = _proj_res(o_s.reshape(m_s, D_MODEL), w_o, layer, xs, tm=m_s)
    qk, v = _qkv(xp, g, w_qk, w_v, 0, 0, gains, tm=1024, out_dtype=BF16)
    win = min(LEFT_CHUNKS * CHUNK, seq)
    x_tail = xp.reshape(batch, seq, D_MODEL)[:, seq - win:].reshape(batch * win, D_MODEL)
    qk_tail, v_tail = _qkv(x_tail, g, w_qk, w_v, 0, 0, gains, tm=batch * win, out_dtype=F32)
    k_new = qk_tail[:, D_MODEL:].reshape(batch, win, N_HEADS, HEAD_DIM)
    v_new = v_tail.reshape(batch, win, N_HEADS, HEAD_DIM)
    o = _band_attn(qk.reshape(batch, seq, 2 * D_MODEL), v.reshape(batch, seq, D_MODEL), _band_bias(table))
    xp = _proj_res(o.reshape(batch * seq, D_MODEL), w_o, layer, xp, tm=1024)
    return xp, xs, k_new, v_new, ks_new, vs_new


def _ssd_stream(x2d, h0_pairs, conv_init, p, *, batch, seq, conv_batch, tm, rows):
    g, w_in, w_out, layer, w_dt, conv_w, conv_b, dt_bias, a_neg, d_skip, gate_g = p
    row_tile = min(1024, batch * seq)
    z, dt = _ssd_z_dt(x2d, g, w_in, layer, w_dt, dt_bias, tm=row_tile)
    xbc, tails = _ssd_xbc(x2d.reshape(conv_batch, -1, D_MODEL), g, w_in, layer, conv_w, conv_b, conv_init, tm=tm)
    xbc = xbc.reshape(batch, seq, CONV_DIM)
    z = z.reshape(batch, seq, D_INNER)
    dt = dt.reshape(batch, seq, LANES)
    pad = (-seq) % SCAN_CHUNK
    if pad:
        widen = lambda t: jnp.pad(t, ((0, 0), (0, pad), (0, 0)))
        xbc, z, dt = widen(xbc), widen(z), widen(dt)
    yn, h_new = _ssd_scan(xbc, dt, z, h0_pairs, a_neg, d_skip, gate_g, rows=rows)
    yn = yn[:, :seq].reshape(batch * seq, D_INNER)
    x2d = _proj_res(yn, w_out, layer, x2d, tm=row_tile)
    conv_state = tails[:, -1, :, SUBLANES - (CONV_WIDTH - 1):].reshape(batch, CONV_WIDTH - 1, CONV_DIM)
    return x2d, _from_pair_layout(h_new), conv_state


def _ssd_layer(xp, xs, state_ssm, state_conv, g, w_in, w_out, layer, w_dt, conv_w, conv_b, dt_bias, a_log,
               d_skip, gate_g, *, batch, seq):
    dec_batch = state_ssm.shape[0]
    dec_seq = xs.shape[0] // dec_batch
    lane_pad = (0, LANES - N_SSM_HEADS)
    p = (g, w_in, w_out, layer, w_dt, conv_w, conv_b,
         jnp.pad(dt_bias.astype(F32), lane_pad).reshape(1, LANES),
         jnp.pad(-jnp.exp(a_log.astype(F32)), lane_pad).reshape(1, LANES),
         jnp.repeat(d_skip, SSM_HEADDIM).reshape(1, D_INNER),
         gate_g.reshape(1, D_INNER))
    conv_init = jnp.pad(state_conv, ((0, 0), (SUBLANES - (CONV_WIDTH - 1), 0), (0, 0)))[None]
    xs, hs, cs = _ssd_stream(xs, _to_pair_layout(state_ssm), conv_init, p, batch=dec_batch, seq=dec_seq,
                             conv_batch=1, tm=dec_batch * dec_seq, rows=SCAN_CHUNK)
    zero_state = jnp.zeros((batch, N_PAIRS, D_STATE, PAIR), F32)
    zero_conv = jnp.zeros((batch, 1, SUBLANES, CONV_DIM), F32)
    xp, hp, cp = _ssd_stream(xp, zero_state, zero_conv, p, batch=batch, seq=seq, conv_batch=batch, tm=1024,
                             rows=2 * SCAN_CHUNK)
    return xp, xs, hp, cp, hs, cs


def kernel(x_prompt, x_sample, cache_k, cache_v, state_ssm, state_conv, ffn_norm, ffn_w_gate_up, ffn_w_down, attn_norm, attn_w_qkv, attn_q_norm, attn_k_norm, attn_rel_bias, attn_w_o, ssd_norm, ssd_w_in, ssd_conv_w, ssd_conv_b, ssd_dt_bias, ssd_a_log, ssd_d_skip, ssd_gate_norm, ssd_w_out):
    batch, seq, _ = x_prompt.shape
    dec_batch, dec_seq, _ = x_sample.shape
    depth = ffn_norm.shape[0]
    xp = x_prompt.reshape(batch * seq, D_MODEL)
    xs = x_sample.reshape(dec_batch * dec_seq, D_MODEL)

    w_o = attn_w_o.astype(BF16)
    w_in = ssd_w_in.astype(BF16)
    w_out = ssd_w_out.astype(BF16)
    w_dt = jnp.pad(w_in[:, :, D_INNER + CONV_DIM:], ((0, 0), (0, 0), (0, LANES - N_SSM_HEADS)))

    kp, vp, hp, cp, ks, vs, hs, cs = [], [], [], [], [], [], [], []
    for i in range(depth):
        j = i // 2
        xp, xs = _ffn_layer(xp, xs, ffn_norm[i, 0], ffn_w_gate_up, ffn_w_down, i, 0)
        if i % 2 == 0:
            xp, xs, k_new, v_new, ks_new, vs_new = _attn_layer(
                xp, xs, cache_k, cache_v, attn_norm[j], attn_w_qkv, w_o, j, attn_q_norm[j], attn_k_norm[j],
                attn_rel_bias[j], batch=batch, seq=seq)
            kp.append(k_new); vp.append(v_new); ks.append(ks_new); vs.append(vs_new)
        else:
            xp, xs, h_p, c_p, h_s, c_s = _ssd_layer(
                xp, xs, state_ssm[j], state_conv[j], ssd_norm[j], w_in, w_out, j,
                w_dt, ssd_conv_w[j], ssd_conv_b[j], ssd_dt_bias[j], ssd_a_log[j],
                ssd_d_skip[j], ssd_gate_norm[j], batch=batch, seq=seq)
            hp.append(h_p); cp.append(c_p); hs.append(h_s); cs.append(c_s)
        xp, xs = _ffn_layer(xp, xs, ffn_norm[i, 1], ffn_w_gate_up, ffn_w_down, i, 1)

    return (xp.reshape(batch, seq, D_MODEL), xs.reshape(dec_batch, dec_seq, D_MODEL),
            jnp.stack(kp), jnp.stack(vp), jnp.stack(hp), jnp.stack(cp),
            jnp.stack(ks), jnp.stack(vs), jnp.stack(hs), jnp.stack(cs))
```

```python
import functools
import math

import jax
import jax.numpy as jnp
import numpy as np
from jax import lax
from jax.experimental import pallas as pl
from jax.experimental.pallas import tpu as pltpu

F32 = jnp.float32
BF16 = jnp.bfloat16

D_MODEL = 2048
CHUNK = 64
LEFT_CHUNKS = 8
HEAD_DIM = 128
N_HEADS = D_MODEL // HEAD_DIM
REL_CLIP = 128
PAST_LEN = 1024
D_INNER = 2 * D_MODEL
SSM_HEADDIM = 64
N_SSM_HEADS = D_INNER // SSM_HEADDIM
N_GROUPS = 8
D_STATE = 128
CONV_WIDTH = 4
CONV_DIM = D_INNER + 2 * N_GROUPS * D_STATE
EPS = 1e-6
NEG_INF = -1e30
LOG2E = math.log2(math.e)

LANES = 128
SUBLANES = 8
MXU_DIM = 256
MIB = 1024 * 1024

Q_BLOCK = 4 * CHUNK
KV_BLOCKS = LEFT_CHUNKS * CHUNK // Q_BLOCK + 1
SCAN_CHUNK = 128
PAIR = 2 * SSM_HEADDIM
N_PAIRS = N_SSM_HEADS // 2
PAIRS_PER_GROUP = N_PAIRS // N_GROUPS
COL_TILE = 1024
assert SCAN_CHUNK == D_STATE == PAIR == LANES

NT_DIMS = (((1,), (1,)), ((), ()))


def _params(semantics, vmem_mib):
    return pltpu.CompilerParams(dimension_semantics=semantics, vmem_limit_bytes=vmem_mib * MIB)


def _rms_scale(x):
    return lax.rsqrt(jnp.mean(x * x, axis=-1, keepdims=True) + EPS)


def _silu(x):
    return x * jax.nn.sigmoid(x)


def _with_norm_on_first(x_ref, g_ref, xn_ref, first, work):
    @pl.when(first)
    def _():
        x = x_ref[...]
        xn_ref[...] = (x * _rms_scale(x) * g_ref[...]).astype(BF16)
        work()

    @pl.when(jnp.logical_not(first))
    def _():
        work()


def _ffn_cast_kernel(x_ref, g_ref, wg_ref, wu_ref, wd_ref, o_ref, wgb_ref, wub_ref, wdb_ref, xn_ref):
    @pl.when(pl.program_id(0) == 0)
    def _():
        x = x_ref[...]
        xn_ref[...] = (x * _rms_scale(x) * g_ref[...]).astype(BF16)
        o_ref[...] = x

    wg = wg_ref[...].astype(BF16)
    wu = wu_ref[...].astype(BF16)
    wd = wd_ref[...].astype(BF16)
    wgb_ref[...] = wg
    wub_ref[...] = wu
    wdb_ref[...] = wd
    xn = xn_ref[...]
    gate = jnp.dot(xn, wg, preferred_element_type=F32)
    up = jnp.dot(xn, wu, preferred_element_type=F32)
    act = (_silu(gate) * up).astype(BF16)
    o_ref[...] += 0.5 * jnp.dot(act, wd, preferred_element_type=F32)


def _ffn_cast(x, g, w_gu, w_d, layer, half, *, tf):
    m, d = x.shape
    d_ff = w_d.shape[2]
    nf = d_ff // tf
    return pl.pallas_call(
        _ffn_cast_kernel,
        grid=(nf,),
        in_specs=[
            pl.BlockSpec((m, d), lambda f: (0, 0)),
            pl.BlockSpec((1, d), lambda f: (0, 0)),
            pl.BlockSpec((None, None, d, tf), lambda f: (layer, half, 0, f)),
            pl.BlockSpec((None, None, d, tf), lambda f: (layer, half, 0, f + nf)),
            pl.BlockSpec((None, None, tf, d), lambda f: (layer, half, f, 0)),
        ],
        out_specs=[
            pl.BlockSpec((m, d), lambda f: (0, 0)),
            pl.BlockSpec((d, tf), lambda f: (0, f)),
            pl.BlockSpec((d, tf), lambda f: (0, f)),
            pl.BlockSpec((tf, d), lambda f: (f, 0)),
        ],
        out_shape=[
            jax.ShapeDtypeStruct((m, d), F32),
            jax.ShapeDtypeStruct((d, d_ff), BF16),
            jax.ShapeDtypeStruct((d, d_ff), BF16),
            jax.ShapeDtypeStruct((d_ff, d), BF16),
        ],
        scratch_shapes=[pltpu.VMEM((m, d), BF16)],
        compiler_params=_params(("arbitrary",), 56),
        name="ffn_cast",
    )(x, g.reshape(1, d), w_gu, w_gu, w_d)


def _ffn_kernel(x_ref, g_ref, wg_ref, wu_ref, wd_ref, o_ref, xn_ref, *, sub):
    def half_down():
        xn = xn_ref[...]
        down = None
        for c in range(wg_ref.shape[1] // sub):
            cs = slice(c * sub, (c + 1) * sub)
            gate = jnp.dot(xn, wg_ref[:, cs], preferred_element_type=F32)
            up = jnp.dot(xn, wu_ref[:, cs], preferred_element_type=F32)
            act = (_silu(gate) * up).astype(BF16)
            part = jnp.dot(act, wd_ref[cs, :], preferred_element_type=F32)
            down = part if down is None else down + part
        return 0.5 * down

    first = pl.program_id(1) == 0

    @pl.when(first)
    def _():
        x = x_ref[...]
        xn_ref[...] = (x * _rms_scale(x) * g_ref[...]).astype(BF16)
        o_ref[...] = x_ref[...] + half_down()

    @pl.when(jnp.logical_not(first))
    def _():
        o_ref[...] += half_down()


def _ffn(x, g, w_g, w_u, w_d, *, tm, tf, sub):
    m, d = x.shape
    d_ff = w_d.shape[0]
    return pl.pallas_call(
        functools.partial(_ffn_kernel, sub=sub),
        grid=(m // tm, d_ff // tf),
        in_specs=[
            pl.BlockSpec((tm, d), lambda i, f: (i, 0)),
            pl.BlockSpec((1, d), lambda i, f: (0, 0)),
            pl.BlockSpec((d, tf), lambda i, f: (0, f)),
            pl.BlockSpec((d, tf), lambda i, f: (0, f)),
            pl.BlockSpec((tf, d), lambda i, f: (f, 0)),
        ],
        out_specs=pl.BlockSpec((tm, d), lambda i, f: (i, 0)),
        out_shape=jax.ShapeDtypeStruct((m, d), F32),
        scratch_shapes=[pltpu.VMEM((tm, d), BF16)],
        compiler_params=_params(("parallel", "arbitrary"), 58),
        name="ffn",
    )(x, g.reshape(1, d), w_g, w_u, w_d)


def _qkv_kernel(x_ref, g_ref, wqk_ref, wv_ref, gain_ref, qk_ref, v_ref, *rest, sub, emit_bf16):
    xn_ref = rest[-1]
    n = pl.program_id(1)

    def work():
        if emit_bf16:
            wqk, wv = rest[0], rest[1]
            wqk[...] = wqk_ref[...].astype(BF16)
            wv[...] = wv_ref[...].astype(BF16)
        else:
            wqk, wv = wqk_ref, wv_ref
        xn = xn_ref[...]
        gain = gain_ref[pl.ds(n // (pl.num_programs(1) // 2), 1), :]
        for c in range(wqk_ref.shape[1] // sub):
            acc = jnp.dot(xn, wqk[:, c * sub:(c + 1) * sub], preferred_element_type=F32)
            for h in range(sub // HEAD_DIM):
                a = acc[:, h * HEAD_DIM:(h + 1) * HEAD_DIM]
                cols = slice(c * sub + h * HEAD_DIM, c * sub + (h + 1) * HEAD_DIM)
                qk_ref[:, cols] = (a * _rms_scale(a) * gain).astype(qk_ref.dtype)
        v_ref[...] = jnp.dot(xn, wv[...], preferred_element_type=F32).astype(v_ref.dtype)

    _with_norm_on_first(x_ref, g_ref, xn_ref, n == 0, work)


def _qkv(x, g, w_qk, w_v, layer, v_col0, gains, *, tm, out_dtype, emit_bf16=False):
    m, d = x.shape
    n_tiles = 2 * D_MODEL // COL_TILE
    tv = D_MODEL // n_tiles
    v0 = v_col0 // tv
    assert not emit_bf16 or m == tm
    out_specs = [
        pl.BlockSpec((tm, COL_TILE), lambda i, n: (i, n)),
        pl.BlockSpec((tm, tv), lambda i, n: (i, n)),
    ]
    out_shape = [
        jax.ShapeDtypeStruct((m, 2 * D_MODEL), out_dtype),
        jax.ShapeDtypeStruct((m, D_MODEL), out_dtype),
    ]
    if emit_bf16:
        out_specs += [pl.BlockSpec((d, COL_TILE), lambda i, n: (0, n)), pl.BlockSpec((d, tv), lambda i, n: (0, n))]
        out_shape += [jax.ShapeDtypeStruct((d, 2 * D_MODEL), BF16), jax.ShapeDtypeStruct((d, D_MODEL), BF16)]
    return pl.pallas_call(
        functools.partial(_qkv_kernel, sub=MXU_DIM, emit_bf16=emit_bf16),
        grid=(m // tm, n_tiles),
        in_specs=[
            pl.BlockSpec((tm, d), lambda i, n: (i, 0)),
            pl.BlockSpec((1, d), lambda i, n: (0, 0)),
            pl.BlockSpec((None, d, COL_TILE), lambda i, n: (layer, 0, n)),
            pl.BlockSpec((None, d, tv), lambda i, n: (layer, 0, v0 + n)),
            pl.BlockSpec((2, HEAD_DIM), lambda i, n: (0, 0)),
        ],
        out_specs=out_specs,
        out_shape=out_shape,
        scratch_shapes=[pltpu.VMEM((tm, d), BF16)],
        compiler_params=_params(("parallel", "arbitrary"), 48),
        name="qkv",
    )(x, g.reshape(1, d), w_qk, w_v, gains)


def _z_dt_kernel(x_ref, g_ref, w_ref, wdt_ref, b_ref, z_ref, dt_ref, xn_ref):
    def project():
        z_ref[...] = jnp.dot(xn_ref[...], w_ref[...], preferred_element_type=F32).astype(z_ref.dtype)

    def project_with_dt():
        raw = jnp.dot(xn_ref[...], wdt_ref[...], preferred_element_type=F32) + b_ref[...]
        dt_ref[...] = jnp.maximum(raw, 0.0) + jnp.log1p(jnp.exp(-jnp.abs(raw)))
        project()

    first = pl.program_id(1) == 0

    @pl.when(first)
    def _():
        x = x_ref[...]
        xn_ref[...] = (x * _rms_scale(x) * g_ref[...]).astype(BF16)
        project_with_dt()

    @pl.when(jnp.logical_not(first))
    def _():
        project()


def _ssd_z_dt(x, g, w_in, layer, w_dt, dt_bias, *, tm):
    m, d = x.shape
    tn = 2 * COL_TILE
    return pl.pallas_call(
        _z_dt_kernel,
        grid=(m // tm, D_INNER // tn),
        in_specs=[
            pl.BlockSpec((tm, d), lambda i, j: (i, 0)),
            pl.BlockSpec((1, d), lambda i, j: (0, 0)),
            pl.BlockSpec((None, d, tn), lambda i, j: (layer, 0, j)),
            pl.BlockSpec((None, d, LANES), lambda i, j: (layer, 0, 0)),
            pl.BlockSpec((1, LANES), lambda i, j: (0, 0)),
        ],
        out_specs=[
            pl.BlockSpec((tm, tn), lambda i, j: (i, j)),
            pl.BlockSpec((tm, LANES), lambda i, j: (i, 0)),
        ],
        out_shape=[
            jax.ShapeDtypeStruct((m, D_INNER), BF16),
            jax.ShapeDtypeStruct((m, LANES), F32),
        ],
        scratch_shapes=[pltpu.VMEM((tm, d), BF16)],
        compiler_params=_params(("parallel", "arbitrary"), 56),
        name="ssd_z_dt",
    )(x, g.reshape(1, d), w_in, w_dt, dt_bias)


def _xbc_kernel(x_ref, g_ref, w_ref, cw_ref, cb_ref, init_ref, o_ref, st_ref, xn_ref, carry_ref, *, tm, sub,
                row_block, streams):
    i = pl.program_id(1)
    n = pl.program_id(2)
    chained = streams == 1

    if chained:
        @pl.when(i == 0)
        def _():
            carry_ref[n] = init_ref[0]

    rb = row_block
    sublane_id = lax.broadcasted_iota(jnp.int32, (1, SUBLANES, sub), 1)

    def work():
        for c in range(w_ref.shape[1] // sub):
            cs = slice(c * sub, (c + 1) * sub)
            taps = [cw_ref[t:t + 1, cs] for t in range(CONV_WIDTH)]
            carry = carry_ref[n, :, cs] if chained else None
            for r in range(tm // rb):
                rows = slice(r * rb, (r + 1) * rb)
                if not chained:
                    carry = init_ref[r, :, cs]
                raw = jnp.dot(xn_ref[rows, :], w_ref[:, cs], preferred_element_type=F32)
                tiles = jnp.concatenate([carry, raw], axis=0).reshape(rb // SUBLANES + 1, SUBLANES, sub)
                conv = cb_ref[:, cs] + raw * taps[CONV_WIDTH - 1]
                conv = conv.reshape(rb // SUBLANES, SUBLANES, sub)
                for t in range(CONV_WIDTH - 1):
                    lag = CONV_WIDTH - 1 - t
                    rot = pltpu.roll(tiles, lag, 1)
                    conv = conv + jnp.where(sublane_id < lag, rot[:-1], rot[1:]) * taps[t]
                o_ref[rows, cs] = _silu(conv).reshape(rb, sub).astype(o_ref.dtype)
                carry = raw[rb - SUBLANES:rb]
                if not chained:
                    st_ref[r, :, cs] = carry
            if chained:
                carry_ref[n, :, cs] = carry
                st_ref[0, :, cs] = carry

    _with_norm_on_first(x_ref, g_ref, xn_ref, n == 0, work)


def _ssd_xbc(x, g, w_in, layer, conv_w, conv_b, init, *, tm):
    b, l, d = x.shape
    streams = init.shape[1]
    tn = 2 * COL_TILE
    n_tiles = CONV_DIM // tn
    tile0 = D_INNER // tn
    assert streams == 1 or l == tm
    rb = min(tm, 512) if streams == 1 else tm // streams
    out_specs = [
        pl.BlockSpec((None, tm, tn), lambda bb, i, n: (bb, i, n)),
        pl.BlockSpec((None, None, streams, SUBLANES, tn), lambda bb, i, n: (bb, i, 0, 0, n)),
    ]
    out_shape = [
        jax.ShapeDtypeStruct((b, l, CONV_DIM), BF16),
        jax.ShapeDtypeStruct((b, l // tm, streams, SUBLANES, CONV_DIM), F32),
    ]
    return pl.pallas_call(
        functools.partial(_xbc_kernel, tm=tm, sub=MXU_DIM, row_block=rb, streams=streams),
        grid=(b, l // tm, n_tiles),
        in_specs=[
            pl.BlockSpec((None, tm, d), lambda bb, i, n: (bb, i, 0)),
            pl.BlockSpec((1, d), lambda bb, i, n: (0, 0)),
            pl.BlockSpec((None, d, tn), lambda bb, i, n: (layer, 0, tile0 + n)),
            pl.BlockSpec((CONV_WIDTH, tn), lambda bb, i, n: (0, n)),
            pl.BlockSpec((1, tn), lambda bb, i, n: (0, n)),
            pl.BlockSpec((None, streams, SUBLANES, tn), lambda bb, i, n: (bb, 0, 0, n)),
        ],
        out_specs=out_specs,
        out_shape=out_shape,
        scratch_shapes=[
            pltpu.VMEM((tm, d), BF16),
            pltpu.VMEM((n_tiles, SUBLANES, tn), F32),
        ],
        compiler_params=_params(("arbitrary", "arbitrary", "arbitrary"), 56),
        name="ssd_xbc",
    )(x, g.reshape(1, d), w_in, conv_w, conv_b.reshape(1, CONV_DIM), init)


def _proj_res_kernel(a_ref, w_ref, x_ref, o_ref):
    o_ref[...] = x_ref[...] + jnp.dot(a_ref[...], w_ref[...], preferred_element_type=F32)


def _proj_res(a, w, layer, x, *, tm):
    m, k = a.shape
    n = w.shape[2]
    return pl.pallas_call(
        _proj_res_kernel,
        grid=(n // COL_TILE, m // tm),
        in_specs=[
            pl.BlockSpec((tm, k), lambda j, i: (i, 0)),
            pl.BlockSpec((None, k, COL_TILE), lambda j, i: (layer, 0, j)),
            pl.BlockSpec((tm, COL_TILE), lambda j, i: (i, j)),
        ],
        out_specs=pl.BlockSpec((tm, COL_TILE), lambda j, i: (i, j)),
        out_shape=jax.ShapeDtypeStruct((m, n), F32),
        compiler_params=_params(("parallel", "parallel"), 56),
        name="proj_res",
    )(a, w, x)


def _band_attn_kernel(q_ref, k0_ref, k1_ref, k2_ref, v0_ref, v1_ref, v2_ref, bias_ref, o_ref):
    k_refs = (k0_ref, k1_ref, k2_ref)
    v_refs = (v0_ref, v1_ref, v2_ref)
    for h in range(N_HEADS):
        cols = slice(h * HEAD_DIM, (h + 1) * HEAD_DIM)
        qh = q_ref[:, cols]
        scores = [lax.dot_general(qh, k_refs[t][:, cols], NT_DIMS, preferred_element_type=F32)
                  + bias_ref[h, :, t * Q_BLOCK:(t + 1) * Q_BLOCK] for t in range(KV_BLOCKS)]
        row_max = jnp.max(functools.reduce(jnp.maximum, scores), axis=-1, keepdims=True)
        probs = [jnp.exp2(s - row_max) for s in scores]
        denom = jnp.sum(functools.reduce(jnp.add, probs), axis=-1, keepdims=True)
        out = functools.reduce(jnp.add, [
            jnp.dot(p.astype(BF16), v_refs[t][:, cols], preferred_element_type=F32)
            for t, p in enumerate(probs)])
        o_ref[:, cols] = (out / denom).astype(o_ref.dtype)


def _band_attn(qk, v, bias):
    b, l, _ = v.shape

    def kv_spec(col, t):
        return pl.BlockSpec((None, Q_BLOCK, D_MODEL),
                            lambda bb, j: (bb, jnp.maximum(j - (KV_BLOCKS - 1 - t), 0), col))

    return pl.pallas_call(
        _band_attn_kernel,
        grid=(b, l // Q_BLOCK),
        in_specs=[pl.BlockSpec((None, Q_BLOCK, D_MODEL), lambda bb, j: (bb, j, 0))]
        + [kv_spec(1, t) for t in range(KV_BLOCKS)]
        + [kv_spec(0, t) for t in range(KV_BLOCKS)]
        + [pl.BlockSpec((None,) + bias.shape[1:], lambda bb, j: (jnp.minimum(j, KV_BLOCKS - 1), 0, 0, 0))],
        out_specs=pl.BlockSpec((None, Q_BLOCK, D_MODEL), lambda bb, j: (bb, j, 0)),
        out_shape=jax.ShapeDtypeStruct((b, l, D_MODEL), BF16),
        compiler_params=_params(("parallel", "parallel"), 56),
        name="band_attn",
    )(qk, qk, qk, qk, v, v, v, bias)


def _step_attn_kernel(qk_ref, v_ref, kc_ref, vc_ref, bias_c_ref, bias_n_ref, o_ref):
    for h in range(N_HEADS):
        cols = slice(h * HEAD_DIM, (h + 1) * HEAD_DIM)
        qh = qk_ref[:, cols].astype(BF16)
        kn = qk_ref[:, D_MODEL + h * HEAD_DIM:D_MODEL + (h + 1) * HEAD_DIM].astype(BF16)
        vn = v_ref[:, cols].astype(BF16)
        kc = kc_ref[:, h, :].astype(BF16)
        vc = vc_ref[:, h, :].astype(BF16)
        s_c = lax.dot_general(qh, kc, NT_DIMS, preferred_element_type=F32) + bias_c_ref[h]
        s_n = lax.dot_general(qh, kn, NT_DIMS, preferred_element_type=F32) + bias_n_ref[h]
        row_max = jnp.maximum(jnp.max(s_c, axis=-1, keepdims=True), jnp.max(s_n, axis=-1, keepdims=True))
        p_c = jnp.exp2(s_c - row_max)
        p_n = jnp.exp2(s_n - row_max)
        denom = jnp.sum(p_c, axis=-1, keepdims=True) + jnp.sum(p_n, axis=-1, keepdims=True)
        out = (jnp.dot(p_c.astype(BF16), vc, preferred_element_type=F32)
               + jnp.dot(p_n.astype(BF16), vn, preferred_element_type=F32))
        o_ref[:, cols] = (out / denom).astype(o_ref.dtype)


def _step_attn(qk, v, k_cache, v_cache, layer, bias_c, bias_n):
    b, s, _ = v.shape
    w = k_cache.shape[2]
    cache_spec = pl.BlockSpec((None, None, w, N_HEADS, HEAD_DIM), lambda i: (layer, i, 0, 0, 0))
    return pl.pallas_call(
        _step_attn_kernel,
        grid=(b,),
        in_specs=[
            pl.BlockSpec((None, s, 2 * D_MODEL), lambda i: (i, 0, 0)),
            pl.BlockSpec((None, s, D_MODEL), lambda i: (i, 0, 0)),
            cache_spec,
            cache_spec,
            pl.BlockSpec(bias_c.shape, lambda i: (0, 0, 0)),
            pl.BlockSpec(bias_n.shape, lambda i: (0, 0, 0)),
        ],
        out_specs=pl.BlockSpec((None, s, D_MODEL), lambda i: (i, 0, 0)),
        out_shape=jax.ShapeDtypeStruct((b, s, D_MODEL), BF16),
        compiler_params=_params(("parallel",), 48),
        name="step_attn",
    )(qk, v, k_cache, v_cache, bias_c, bias_n)


def _lag_vector(table, rows, width, dist0):
    period = width + rows
    m = np.arange(period)
    lag = np.where(m < width, m, m - period)
    idx = np.clip(dist0 - lag, -REL_CLIP, REL_CLIP) + REL_CLIP
    return table[:, idx] * LOG2E


def _skewed_bias(table, rows, width, dist0):
    u = _lag_vector(table, rows, width, dist0)
    period = u.shape[1]
    return jnp.tile(u, (1, rows))[:, :rows * (period - 1)].reshape(-1, rows, period - 1)[:, :, :width]


def _band_bias_kernel(u_ref, o_ref):
    n_variants, rows, width = o_ref.shape
    skew = pltpu.roll(jnp.broadcast_to(u_ref[...], (rows, u_ref.shape[-1])), 0, 1, stride=1, stride_axis=0)
    skew = skew[:, :width]
    qi = lax.broadcasted_iota(jnp.int32, (rows, width), 0)
    kn = lax.broadcasted_iota(jnp.int32, (rows, width), 1)
    chunk_gap = qi // CHUNK + (n_variants - 1) * (rows // CHUNK) - kn // CHUNK
    in_band = (chunk_gap >= 0) & (chunk_gap <= LEFT_CHUNKS)
    for v in range(n_variants):
        visible = in_band & (kn // rows >= n_variants - 1 - v)
        o_ref[v] = jnp.where(visible, skew, NEG_INF)


def _band_bias(table):
    width = KV_BLOCKS * Q_BLOCK
    u = _lag_vector(table, Q_BLOCK, width, (KV_BLOCKS - 1) * Q_BLOCK).astype(F32)
    n_heads, period = u.shape
    return pl.pallas_call(
        _band_bias_kernel,
        grid=(n_heads,),
        in_specs=[pl.BlockSpec((None, 1, period), lambda h: (h, 0, 0))],
        out_specs=pl.BlockSpec((KV_BLOCKS, None, Q_BLOCK, width), lambda h: (0, h, 0, 0)),
        out_shape=jax.ShapeDtypeStruct((KV_BLOCKS, n_heads, Q_BLOCK, width), F32),
        compiler_params=_params(("parallel",), 32),
        name="band_bias",
    )(u.reshape(n_heads, 1, period))


def _step_bias(table, s, w):
    q_pos = PAST_LEN + np.arange(s)
    k_pos = np.concatenate([PAST_LEN - w + np.arange(w), q_pos])
    qch, kch = q_pos // CHUNK, k_pos // CHUNK
    mask = (kch[None, :] <= qch[:, None]) & (kch[None, :] >= qch[:, None] - LEFT_CHUNKS)
    bias_c = jnp.where(mask[None, :, :w], _skewed_bias(table, s, w, w), NEG_INF).astype(F32)
    bias_n = jnp.where(mask[None, :, w:], _skewed_bias(table, s, s, 0), NEG_INF).astype(F32)
    return bias_c, bias_n


def _ssd_scan_kernel(xs_ref, b_ref, c_ref, dt_ref, z_ref, h0_ref, a_ref, dskip_ref, gg_ref,
                     yn_ref, hout_ref, state_ref, y_ref, src_t_ref, w_t_ref, *, rows):
    lc = SCAN_CHUNK
    i = pl.program_id(1)

    @pl.when(i == 0)
    def _():
        state_ref[...] = h0_ref[...]

    row_id = lax.broadcasted_iota(jnp.int32, (lc, lc), 0)
    col_id = lax.broadcasted_iota(jnp.int32, (lc, lc), 1)
    causal = col_id <= row_id
    causal_f = causal.astype(F32)
    low_half = lax.broadcasted_iota(jnp.int32, (lc, PAIR), 1) < SSM_HEADDIM
    low_half_row = lax.broadcasted_iota(jnp.int32, (1, PAIR), 1) < SSM_HEADDIM

    def chunk(c, carry):
        rows_c = slice(c * lc, (c + 1) * lc)
        dt = dt_ref[rows_c, :]
        a_cs = jnp.dot(causal_f, dt * (a_ref[...] * LOG2E), precision=lax.Precision.HIGHEST,
                       preferred_element_type=F32)
        a_cs_t = a_cs.T
        dt_t = dt.T
        src_t_ref[...] = a_cs_t - jnp.log2(dt_t)
        w_t_ref[...] = dt_t * jnp.exp2(a_cs_t[:, lc - 1:lc] - a_cs_t)
        chunk_decay = jnp.exp2(a_cs[lc - 1:lc, :])
        for g in range(N_GROUPS):
            gcols = slice(g * D_STATE, (g + 1) * D_STATE)
            bg = b_ref[rows_c, gcols]
            cg = c_ref[rows_c, gcols]
            cb = lax.dot_general(cg, bg, NT_DIMS, preferred_element_type=F32)
            bg_t = bg.astype(F32).T
            cg_f = cg.astype(F32)
            for jp in range(PAIRS_PER_GROUP):
                q = g * PAIRS_PER_GROUP + jp
                pcols = slice(q * PAIR, (q + 1) * PAIR)
                xp = xs_ref[rows_c, pcols]
                h_t = state_ref[q]
                lhs_y, lhs_s = [], []
                for h in (2 * q, 2 * q + 1):
                    col = jnp.broadcast_to(a_cs[:, h:h + 1], (lc, lc))
                    row = jnp.broadcast_to(src_t_ref[h:h + 1, :], (lc, lc))
                    within = cb * jnp.exp2(jnp.where(causal, col - row, -jnp.inf))
                    carried = cg_f * jnp.exp2(col)
                    lhs_y.append(jnp.concatenate([within.astype(BF16), carried.astype(BF16)], axis=1))
                    lhs_s.append((bg_t * jnp.broadcast_to(w_t_ref[h:h + 1, :], (D_STATE, lc))).astype(BF16))
                y2 = jnp.dot(jnp.concatenate(lhs_y, axis=0), jnp.concatenate([xp, h_t.astype(BF16)], axis=0),
                             preferred_element_type=F32)
                s2 = jnp.dot(jnp.concatenate(lhs_s, axis=0), xp, preferred_element_type=F32)
                decay = jnp.where(low_half_row,
                                  jnp.broadcast_to(chunk_decay[:, 2 * q:2 * q + 1], (1, PAIR)),
                                  jnp.broadcast_to(chunk_decay[:, 2 * q + 1:2 * q + 2], (1, PAIR)))
                state_ref[q] = h_t * decay + jnp.where(low_half, s2[:D_STATE], s2[D_STATE:])
                y_ref[rows_c, pcols] = (jnp.where(low_half, y2[:lc], y2[lc:])
                                        + dskip_ref[:, pcols] * xp.astype(F32))
        return carry

    for c in range(rows // lc):
        chunk(c, 0)

    y = y_ref[...] * _silu(z_ref[...].astype(F32))
    yn_ref[...] = (y * _rms_scale(y) * gg_ref[...]).astype(yn_ref.dtype)

    @pl.when(i == pl.num_programs(1) - 1)
    def _():
        hout_ref[...] = state_ref[...]


def _ssd_scan(xbc, dt, z, h0, a_neg, d_skip, gate_g, *, rows):
    b, l, _ = xbc.shape
    gn = N_GROUPS * D_STATE
    b_block = D_INNER // gn
    return pl.pallas_call(
        functools.partial(_ssd_scan_kernel, rows=rows),
        grid=(b, l // rows),
        in_specs=[
            pl.BlockSpec((None, rows, D_INNER), lambda bb, i: (bb, i, 0)),
            pl.BlockSpec((None, rows, gn), lambda bb, i: (bb, i, b_block)),
            pl.BlockSpec((None, rows, gn), lambda bb, i: (bb, i, b_block + 1)),
            pl.BlockSpec((None, rows, LANES), lambda bb, i: (bb, i, 0)),
            pl.BlockSpec((None, rows, D_INNER), lambda bb, i: (bb, i, 0)),
            pl.BlockSpec((None, N_PAIRS, D_STATE, PAIR), lambda bb, i: (bb, 0, 0, 0)),
            pl.BlockSpec((1, LANES), lambda bb, i: (0, 0)),
            pl.BlockSpec((1, D_INNER), lambda bb, i: (0, 0)),
            pl.BlockSpec((1, D_INNER), lambda bb, i: (0, 0)),
        ],
        out_specs=[
            pl.BlockSpec((None, rows, D_INNER), lambda bb, i: (bb, i, 0)),
            pl.BlockSpec((None, N_PAIRS, D_STATE, PAIR), lambda bb, i: (bb, 0, 0, 0)),
        ],
        out_shape=[
            jax.ShapeDtypeStruct((b, l, D_INNER), BF16),
            jax.ShapeDtypeStruct((b, N_PAIRS, D_STATE, PAIR), F32),
        ],
        scratch_shapes=[
            pltpu.VMEM((N_PAIRS, D_STATE, PAIR), F32),
            pltpu.VMEM((rows, D_INNER), F32),
            pltpu.VMEM((LANES, SCAN_CHUNK), F32),
            pltpu.VMEM((LANES, SCAN_CHUNK), F32),
        ],
        compiler_params=_params(("arbitrary", "arbitrary"), 48),
        name="ssd_scan",
    )(xbc, xbc, xbc, dt, z, h0, a_neg, d_skip, gate_g)


def _to_pair_layout(h):
    b = h.shape[0]
    return h.reshape(b, N_PAIRS, 2, SSM_HEADDIM, D_STATE).transpose(0, 1, 4, 2, 3).reshape(b, N_PAIRS, D_STATE, PAIR)


def _from_pair_layout(h):
    b = h.shape[0]
    return (h.reshape(b, N_PAIRS, D_STATE, 2, SSM_HEADDIM).transpose(0, 1, 3, 4, 2)
            .reshape(b, N_SSM_HEADS, SSM_HEADDIM, D_STATE))


def _ffn_layer(xp, xs, g, w_gu, w_d, layer, half):
    xs, w_g, w_u, w_dn = _ffn_cast(xs, g, w_gu, w_d, layer, half, tf=512)
    xp = _ffn(xp, g, w_g, w_u, w_dn, tm=1024, tf=512, sub=MXU_DIM)
    return xp, xs


def _attn_layer(xp, xs, cache_k, cache_v, g, w_qkv, w_o, layer, q_gain, k_gain, table, *, batch, seq):
    dec_batch, w = cache_k.shape[1], cache_k.shape[2]
    dec_seq = xs.shape[0] // dec_batch
    gains = jnp.stack([q_gain * (HEAD_DIM ** -0.5 * LOG2E), k_gain]).astype(F32)
    m_s = dec_batch * dec_seq
    qk_s, v_s, w_qk, w_v = _qkv(xs, g, w_qkv, w_qkv, layer, 2 * D_MODEL, gains, tm=m_s, out_dtype=F32,
                                emit_bf16=True)
    w_qk, w_v = w_qk[None], w_v[None]
    ks_new = qk_s[:, D_MODEL:].reshape(dec_batch, dec_seq, N_HEADS, HEAD_DIM)
    vs_new = v_s.reshape(dec_batch, dec_seq, N_HEADS, HEAD_DIM)
    bias_c, bias_n = _step_bias(table, dec_seq, w)
    o_s = _step_attn(qk_s.reshape(dec_batch, dec_seq, 2 * D_MODEL), v_s.reshape(dec_batch, dec_seq, D_MODEL),
                     cache_k, cache_v, layer, bias_c, bias_n)
    xs = _proj_res(o_s.reshape(m_s, D_MODEL), w_o, layer, xs, tm=m_s)
    qk, v = _qkv(xp, g, w_qk, w_v, 0, 0, gains, tm=1024, out_dtype=BF16)
    win = min(LEFT_CHUNKS * CHUNK, seq)
    x_tail = xp.reshape(batch, seq, D_MODEL)[:, seq - win:].reshape(batch * win, D_MODEL)
    qk_tail, v_tail = _qkv(x_tail, g, w_qk, w_v, 0, 0, gains, tm=batch * win, out_dtype=F32)
    k_new = qk_tail[:, D_MODEL:].reshape(batch, win, N_HEADS, HEAD_DIM)
    v_new = v_tail.reshape(batch, win, N_HEADS, HEAD_DIM)
    o = _band_attn(qk.reshape(batch, seq, 2 * D_MODEL), v.reshape(batch, seq, D_MODEL), _band_bias(table))
    xp = _proj_res(o.reshape(batch * seq, D_MODEL), w_o, layer, xp, tm=1024)
    return xp, xs, k_new, v_new, ks_new, vs_new


def _ssd_stream(x2d, h0_pairs, conv_init, p, *, batch, seq, conv_batch, tm, rows):
    g, w_in, w_out, layer, w_dt, conv_w, conv_b, dt_bias, a_neg, d_skip, gate_g = p
    row_tile = min(1024, batch * seq)
    z, dt = _ssd_z_dt(x2d, g, w_in, layer, w_dt, dt_bias, tm=row_tile)
    xbc, tails = _ssd_xbc(x2d.reshape(conv_batch, -1, D_MODEL), g, w_in, layer, conv_w, conv_b, conv_init, tm=tm)
    xbc = xbc.reshape(batch, seq, CONV_DIM)
    z = z.reshape(batch, seq, D_INNER)
    dt = dt.reshape(batch, seq, LANES)
    pad = (-seq) % SCAN_CHUNK
    if pad:
        widen = lambda t: jnp.pad(t, ((0, 0), (0, pad), (0, 0)))
        xbc, z, dt = widen(xbc), widen(z), widen(dt)
    yn, h_new = _ssd_scan(xbc, dt, z, h0_pairs, a_neg, d_skip, gate_g, rows=rows)
    yn = yn[:, :seq].reshape(batch * seq, D_INNER)
    x2d = _proj_res(yn, w_out, layer, x2d, tm=row_tile)
    conv_state = tails[:, -1, :, SUBLANES - (CONV_WIDTH - 1):].reshape(batch, CONV_WIDTH - 1, CONV_DIM)
    return x2d, _from_pair_layout(h_new), conv_state


def _ssd_layer(xp, xs, state_ssm, state_conv, g, w_in, w_out, layer, w_dt, conv_w, conv_b, dt_bias, a_log,
               d_skip, gate_g, *, batch, seq):
    dec_batch = state_ssm.shape[0]
    dec_seq = xs.shape[0] // dec_batch
    lane_pad = (0, LANES - N_SSM_HEADS)
    p = (g, w_in, w_out, layer, w_dt, conv_w, conv_b,
         jnp.pad(dt_bias.astype(F32), lane_pad).reshape(1, LANES),
         jnp.pad(-jnp.exp(a_log.astype(F32)), lane_pad).reshape(1, LANES),
         jnp.repeat(d_skip, SSM_HEADDIM).reshape(1, D_INNER),
         gate_g.reshape(1, D_INNER))
    conv_init = jnp.pad(state_conv, ((0, 0), (SUBLANES - (CONV_WIDTH - 1), 0), (0, 0)))[None]
    xs, hs, cs = _ssd_stream(xs, _to_pair_layout(state_ssm), conv_init, p, batch=dec_batch, seq=dec_seq,
                             conv_batch=1, tm=dec_batch * dec_seq, rows=SCAN_CHUNK)
    zero_state = jnp.zeros((batch, N_PAIRS, D_STATE, PAIR), F32)
    zero_conv = jnp.zeros((batch, 1, SUBLANES, CONV_DIM), F32)
    xp, hp, cp = _ssd_stream(xp, zero_state, zero_conv, p, batch=batch, seq=seq, conv_batch=batch, tm=1024,
                             rows=2 * SCAN_CHUNK)
    return xp, xs, hp, cp, hs, cs


def kernel(x_prompt, x_sample, cache_k, cache_v, state_ssm, state_conv, ffn_norm, ffn_w_gate_up, ffn_w_down, attn_norm, attn_w_qkv, attn_q_norm, attn_k_norm, attn_rel_bias, attn_w_o, ssd_norm, ssd_w_in, ssd_conv_w, ssd_conv_b, ssd_dt_bias, ssd_a_log, ssd_d_skip, ssd_gate_norm, ssd_w_out):
    batch, seq, _ = x_prompt.shape
    dec_batch, dec_seq, _ = x_sample.shape
    depth = ffn_norm.shape[0]
    xp = x_prompt.reshape(batch * seq, D_MODEL)
    xs = x_sample.reshape(dec_batch * dec_seq, D_MODEL)

    w_o = attn_w_o.astype(BF16)
    w_in = ssd_w_in.astype(BF16)
    w_out = ssd_w_out.astype(BF16)
    w_dt = jnp.pad(w_in[:, :, D_INNER + CONV_DIM:], ((0, 0), (0, 0), (0, LANES - N_SSM_HEADS)))

    kp, vp, hp, cp, ks, vs, hs, cs = [], [], [], [], [], [], [], []
    for i in range(depth):
        j = i // 2
        xp, xs = _ffn_layer(xp, xs, ffn_norm[i, 0], ffn_w_gate_up, ffn_w_down, i, 0)
        if i % 2 == 0:
            xp, xs, k_new, v_new, ks_new, vs_new = _attn_layer(
                xp, xs, cache_k, cache_v, attn_norm[j], attn_w_qkv, w_o, j, attn_q_norm[j], attn_k_norm[j],
                attn_rel_bias[j], batch=batch, seq=seq)
            kp.append(k_new); vp.append(v_new); ks.append(ks_new); vs.append(vs_new)
        else:
            xp, xs, h_p, c_p, h_s, c_s = _ssd_layer(
                xp, xs, state_ssm[j], state_conv[j], ssd_norm[j], w_in, w_out, j,
                w_dt, ssd_conv_w[j], ssd_conv_b[j], ssd_dt_bias[j], ssd_a_log[j],
                ssd_d_skip[j], ssd_gate_norm[j], batch=batch, seq=seq)
            hp.append(h_p); cp.append(c_p); hs.append(h_s); cs.append(c_s)
        xp, xs = _ffn_layer(xp, xs, ffn_norm[i, 1], ffn_w_gate_up, ffn_w_down, i, 1)

    return (xp.reshape(batch, seq, D_MODEL), xs.reshape(dec_batch, dec_seq, D_MODEL),
            jnp.stack(kp), jnp.stack(vp), jnp.stack(hp), jnp.stack(cp),
            jnp.stack(ks), jnp.stack(vs), jnp.stack(hs), jnp.stack(cs))
```
